```python
import math
import jax
import jax.numpy as jnp
from jax import lax
import numpy as np

D_MODEL = 4096
BATCH = 4
SEQ = 2048
DEPTH = 2

F32 = jnp.float32
EPS = 1e-6
NEG_INF = -1e30

N_MIXERS = 4
MIX_WIDTH = D_MODEL
GROUP_WIDTH = MIX_WIDTH // N_MIXERS

S5_CH_PER_GROUP = 16
S5_GROUPS = GROUP_WIDTH // S5_CH_PER_GROUP
S5_STATE = 64

DIFF_HEADS = 8
DIFF_HEAD_DIM = GROUP_WIDTH // (2 * DIFF_HEADS)
Q_BLOCK = 128

MOBA_HEADS = 8
MOBA_HEAD_DIM = GROUP_WIDTH // MOBA_HEADS
MOBA_BLOCK = 256
MOBA_TOPK = 3
MOBA_Q_CHUNK = 32

SSD_HEAD_DIM = 64
SSD_HEADS = GROUP_WIDTH // SSD_HEAD_DIM
SSD_GROUPS = 4
SSD_STATE = 128
SSD_CONV = 4
SSD_CHUNK = 128
SSD_BC = SSD_GROUPS * SSD_STATE
SSD_XBC = GROUP_WIDTH + 2 * SSD_BC

REL_BUCKETS = 32
REL_MAX_DIST = 128
ATTN_HEADS = DIFF_HEADS + MOBA_HEADS

D_FF = 4 * D_MODEL

OFF_S5 = 0
OFF_DIFF = OFF_S5 + GROUP_WIDTH
OFF_MOBA = OFF_DIFF + 3 * GROUP_WIDTH
OFF_SSD = OFF_MOBA + 3 * GROUP_WIDTH
IN_WIDTH = OFF_SSD + GROUP_WIDTH + SSD_XBC + SSD_HEADS

kernel_name = 'hybrid_parallel_heads_s5_diffattn_moba_ssd'


def rms_norm(x, w):
    xf = x.astype(F32)
    y = xf * lax.rsqrt(jnp.mean(xf * xf, axis=-1, keepdims=True) + EPS)
    return (y * w.astype(F32)).astype(x.dtype)


def rel_bucket(dist):
    n = jnp.maximum(dist, 0)
    max_exact = REL_BUCKETS // 2
    log_ratio = jnp.log(jnp.maximum(n, 1).astype(F32) / max_exact) / math.log(REL_MAX_DIST / max_exact)
    large = max_exact + (log_ratio * (REL_BUCKETS - max_exact)).astype(jnp.int32)
    large = jnp.minimum(large, REL_BUCKETS - 1)
    return jnp.where(n < max_exact, n, large)


def _s5_combine(e1, e2):
    a1r, a1i, b1r, b1i = e1
    a2r, a2i, b2r, b2i = e2
    ar = a2r * a1r - a2i * a1i
    ai = a2r * a1i + a2i * a1r
    br = a2r * b1r - a2i * b1i + b2r
    bi = a2r * b1i + a2i * b1r + b2i
    return ar, ai, br, bi


def s5_mixer(u, lam_re, lam_im, log_dt, b_re, b_im, c_re, c_im, d_skip, w_glu):
    bsz, seq, _ = u.shape
    G, Hc, P = S5_GROUPS, S5_CH_PER_GROUP, S5_STATE
    uf = u.astype(F32).reshape(bsz, seq, G, Hc)
    dt = jnp.exp(log_dt.astype(F32))[:, None]
    lr, li = lam_re.astype(F32), lam_im.astype(F32)
    mag = jnp.exp(lr * dt)
    ab_re = mag * jnp.cos(li * dt)
    ab_im = mag * jnp.sin(li * dt)
    den = lr * lr + li * li
    f_re = ((ab_re - 1.0) * lr + ab_im * li) / den
    f_im = (ab_im * lr - (ab_re - 1.0) * li) / den
    br, bi = b_re.astype(F32), b_im.astype(F32)
    bb_re = f_re[..., None] * br - f_im[..., None] * bi
    bb_im = f_re[..., None] * bi + f_im[..., None] * br
    bu_re = jnp.einsum('bsgh,gph->bsgp', uf, bb_re)
    bu_im = jnp.einsum('bsgh,gph->bsgp', uf, bb_im)
    a_re = jnp.broadcast_to(ab_re, (1, seq, G, P))
    a_im = jnp.broadcast_to(ab_im, (1, seq, G, P))
    _, _, s_re, s_im = lax.associative_scan(_s5_combine, (a_re, a_im, bu_re, bu_im), axis=1)
    y = (jnp.einsum('bsgp,ghp->bsgh', s_re, c_re.astype(F32))
         - jnp.einsum('bsgp,ghp->bsgh', s_im, c_im.astype(F32))
         + d_skip.astype(F32) * uf)
    y = jax.nn.gelu(y.reshape(bsz, seq, GROUP_WIDTH)).astype(u.dtype)
    return y * jax.nn.sigmoid(y @ w_glu)


def diff_attention(q, k, v, lam_q1, lam_k1, lam_q2, lam_k2, subln_w, rel_table, layer_idx):
    bsz, seq, _ = q.shape
    H, dh = DIFF_HEADS, DIFF_HEAD_DIM
    q = q.reshape(bsz, seq, H, 2, dh).transpose(0, 2, 3, 1, 4)
    k = k.reshape(bsz, seq, H, 2, dh).transpose(0, 2, 3, 1, 4)
    v = v.reshape(bsz, seq, H, 2 * dh).transpose(0, 2, 1, 3)
    lam_init = 0.8 - 0.6 * math.exp(-0.3 * layer_idx)
    lam = (jnp.exp(jnp.sum(lam_q1.astype(F32) * lam_k1.astype(F32)))
           - jnp.exp(jnp.sum(lam_q2.astype(F32) * lam_k2.astype(F32))) + lam_init)
    scale = dh ** -0.5
    k_pos = jnp.arange(seq)
    tbl = rel_table[:, :H].astype(F32)
    nqb = seq // Q_BLOCK
    q_blocks = jnp.moveaxis(q.reshape(bsz, H, 2, nqb, Q_BLOCK, dh), 3, 0)

    def one_block(args):
        qi, qb = args
        q_pos = qi * Q_BLOCK + jnp.arange(Q_BLOCK)
        logits = jnp.einsum('bhmqd,bhmkd->bhmqk', qb, k).astype(F32) * scale
        bias = jnp.transpose(tbl[rel_bucket(q_pos[:, None] - k_pos[None, :])], (2, 0, 1))
        causal = k_pos[None, :] <= q_pos[:, None]
        logits = jnp.where(causal, logits + bias[None, :, None], NEG_INF)
        p = jax.nn.softmax(logits, axis=-1)
        w = p[:, :, 0] - lam * p[:, :, 1]
        return jnp.einsum('bhqk,bhkd->bhqd', w.astype(v.dtype), v)

    out = lax.map(one_block, (jnp.arange(nqb), q_blocks))
    out = jnp.moveaxis(out, 0, 2).reshape(bsz, H, seq, 2 * dh)
    out = rms_norm(out, subln_w) * (1.0 - lam_init)
    return out.transpose(0, 2, 1, 3).reshape(bsz, seq, H * 2 * dh)


def moba_attention(q, k, v, rel_table):
    bsz, seq, _ = q.shape
    H, dh, BLK, QC = MOBA_HEADS, MOBA_HEAD_DIM, MOBA_BLOCK, MOBA_Q_CHUNK
    q = q.reshape(bsz, seq, H, dh).transpose(0, 2, 1, 3)
    k = k.reshape(bsz, seq, H, dh).transpose(0, 2, 1, 3)
    v = v.reshape(bsz, seq, H, dh).transpose(0, 2, 1, 3)
    nb = -(-seq // BLK)
    pad = nb * BLK - seq
    k = jnp.pad(k, ((0, 0), (0, 0), (0, pad), (0, 0)))
    v = jnp.pad(v, ((0, 0), (0, 0), (0, pad), (0, 0)))
    k_blocks = k.reshape(bsz, H, nb, BLK, dh)
    v_blocks = v.reshape(bsz, H, nb, BLK, dh)
    k_mean = jnp.mean(k_blocks.astype(F32), axis=3)
    topk = max(1, min(MOBA_TOPK, nb - 1))
    tbl = rel_table[:, DIFF_HEADS:].astype(F32).T
    scale = dh ** -0.5
    b_idx = jnp.arange(bsz)[:, None, None, None]
    h_idx = jnp.arange(H)[None, :, None, None]
    offs = jnp.arange(BLK)
    nqc = seq // QC
    q_chunks = jnp.moveaxis(q.reshape(bsz, H, nqc, QC, dh), 2, 0)

    def one_chunk(args):
        ci, qc = args
        q_pos = ci * QC + jnp.arange(QC)
        own = (ci * QC) // BLK
        gate = jnp.einsum('bhqd,bhnd->bhqn', qc.astype(F32), k_mean)
        gate = jnp.where(jnp.arange(nb) < own, gate, NEG_INF)
        _, sel = lax.top_k(gate, topk)
        valid = sel < own
        k_sel = k_blocks[b_idx, h_idx, sel]
        v_sel = v_blocks[b_idx, h_idx, sel]
        k_pos_sel = sel[..., None] * BLK + offs
        logit_sel = jnp.einsum('bhqd,bhqtkd->bhqtk', qc, k_sel).astype(F32) * scale
        logit_sel = logit_sel + tbl[h_idx[..., None], rel_bucket(q_pos[:, None, None] - k_pos_sel)]
        logit_sel = jnp.where(valid[..., None], logit_sel, NEG_INF).reshape(bsz, H, QC, topk * BLK)
        k_own = lax.dynamic_slice_in_dim(k, own * BLK, BLK, axis=2)
        v_own = lax.dynamic_slice_in_dim(v, own * BLK, BLK, axis=2)
        k_pos_own = own * BLK + offs
        logit_own = jnp.einsum('bhqd,bhkd->bhqk', qc, k_own).astype(F32) * scale
        logit_own = logit_own + tbl[:, rel_bucket(q_pos[:, None] - k_pos_own[None, :])][None]
        logit_own = jnp.where(k_pos_own[None, :] <= q_pos[:, None], logit_own, NEG_INF)
        p = jax.nn.softmax(jnp.concatenate([logit_sel, logit_own], axis=-1), axis=-1).astype(v.dtype)
        p_sel, p_own = p[..., :topk * BLK], p[..., topk * BLK:]
        return (jnp.einsum('bhqt,bhqtd->bhqd', p_sel, v_sel.reshape(bsz, H, QC, topk * BLK, dh))
                + jnp.einsum('bhqk,bhkd->bhqd', p_own, v_own))

    out = lax.map(one_chunk, (jnp.arange(nqc), q_chunks))
    out = jnp.moveaxis(out, 0, 2).reshape(bsz, H, seq, dh)
    return out.transpose(0, 2, 1, 3).reshape(bsz, seq, H * dh)


def ssd_chunked(x, a, b, c):
    bsz, seq, H, P = x.shape
    G, N, Q = SSD_GROUPS, SSD_STATE, SSD_CHUNK
    R = H // G
    nc = seq // Q
    x = x.reshape(bsz, nc, Q, G, R, P)
    b = b.reshape(bsz, nc, Q, G, N)
    c = c.reshape(bsz, nc, Q, G, N)
    a = a.reshape(bsz, nc, Q, G, R).transpose(0, 3, 4, 1, 2)
    a_cum = jnp.cumsum(a, axis=-1)
    causal = jnp.tril(jnp.ones((Q, Q), dtype=bool))
    seg = jnp.where(causal, a_cum[..., :, None] - a_cum[..., None, :], NEG_INF)
    decay_in = jnp.exp(seg)
    y_diag = jnp.einsum('bclgn,bcsgn,bgrcls,bcsgrp->bclgrp', c, b, decay_in, x)
    decay_to_end = jnp.exp(a_cum[..., -1:] - a_cum)
    chunk_states = jnp.einsum('bclgn,bgrcl,bclgrp->bcgrpn', b, decay_to_end, x)
    chunk_decay = jnp.exp(a_cum[..., -1])

    def step(h, inp):
        s, d = inp
        return h * d[..., None, None] + s, h

    h0 = jnp.zeros((bsz, G, R, P, N), F32)
    _, prev = lax.scan(step, h0, (jnp.moveaxis(chunk_states, 1, 0), jnp.moveaxis(chunk_decay, 3, 0)))
    prev = jnp.moveaxis(prev, 0, 1)
    y_off = jnp.einsum('bclgn,bcgrpn,bgrcl->bclgrp', c, prev, jnp.exp(a_cum))
    return (y_diag + y_off).reshape(bsz, seq, H, P)


def ssd_mixer(z, xbc, dt_raw, conv_w, conv_b, dt_bias, a_log, d_skip, norm_w):
    bsz, seq, _ = z.shape
    xbc = lax.conv_general_dilated(xbc, conv_w, window_strides=(1,), padding=[(SSD_CONV - 1, 0)],
                                   dimension_numbers=('NWC', 'WIO', 'NWC'),
                                   feature_group_count=SSD_XBC) + conv_b
    xbc = jax.nn.silu(xbc)
    xs = xbc[..., :GROUP_WIDTH].reshape(bsz, seq, SSD_HEADS, SSD_HEAD_DIM).astype(F32)
    bs = xbc[..., GROUP_WIDTH:GROUP_WIDTH + SSD_BC].reshape(bsz, seq, SSD_GROUPS, SSD_STATE).astype(F32)
    cs = xbc[..., GROUP_WIDTH + SSD_BC:].reshape(bsz, seq, SSD_GROUPS, SSD_STATE).astype(F32)
    dt = jax.nn.softplus(dt_raw.astype(F32) + dt_bias.astype(F32))
    a = -jnp.exp(a_log.astype(F32))
    y = ssd_chunked(xs * dt[..., None], dt * a, bs, cs)
    y = y + d_skip.astype(F32)[:, None] * xs
    y = y.reshape(bsz, seq, GROUP_WIDTH) * jax.nn.silu(z.astype(F32))
    y = rms_norm(y.reshape(bsz, seq, SSD_GROUPS, GROUP_WIDTH // SSD_GROUPS),
                 norm_w.reshape(SSD_GROUPS, GROUP_WIDTH // SSD_GROUPS))
    return y.reshape(bsz, seq, GROUP_WIDTH).astype(z.dtype)


def setup_inputs(seed: int = 0) -> dict:
    key = jax.random.key(seed)
    keys = iter(list(jax.random.split(key, 40)))
    L = DEPTH

    def nrm(shape, scale):
        return jax.random.normal(next(keys), shape, F32) * scale

    def gain(shape):
        return 1.0 + nrm(shape, 0.02)

    x = nrm((BATCH, SEQ, D_MODEL), 1.0)
    rel_bias_table = nrm((REL_BUCKETS, ATTN_HEADS), 0.2)
    attn_norm_w = gain((L, D_MODEL))
    w_in = nrm((L, D_MODEL, IN_WIDTH), D_MODEL ** -0.5)
    s5_lam_re = -0.5 + nrm((L, S5_GROUPS, S5_STATE), 0.01)
    s5_lam_im = math.pi * jnp.arange(S5_STATE, dtype=F32) + nrm((L, S5_GROUPS, S5_STATE), 0.01)
    s5_log_dt = jax.random.uniform(next(keys), (L, S5_GROUPS), F32, math.log(1e-3), math.log(1e-1))
    s5_b_re = nrm((L, S5_GROUPS, S5_STATE, S5_CH_PER_GROUP), S5_CH_PER_GROUP ** -0.5)
    s5_b_im = nrm((L, S5_GROUPS, S5_STATE, S5_CH_PER_GROUP), S5_CH_PER_GROUP ** -0.5)
    s5_c_re = nrm((L, S5_GROUPS, S5_CH_PER_GROUP, S5_STATE), S5_STATE ** -0.5)
    s5_c_im = nrm((L, S5_GROUPS, S5_CH_PER_GROUP, S5_STATE), S5_STATE ** -0.5)
    s5_d = nrm((L, S5_GROUPS, S5_CH_PER_GROUP), 0.5)
    s5_w_glu = nrm((L, GROUP_WIDTH, GROUP_WIDTH), GROUP_WIDTH ** -0.5)
    s5_out_norm_w = gain((L, GROUP_WIDTH))
    diff_lam_q1 = nrm((L, DIFF_HEAD_DIM), 0.1)
    diff_lam_k1 = nrm((L, DIFF_HEAD_DIM), 0.1)
    diff_lam_q2 = nrm((L, DIFF_HEAD_DIM), 0.1)
    diff_lam_k2 = nrm((L, DIFF_HEAD_DIM), 0.1)
    diff_subln_w = gain((L, 2 * DIFF_HEAD_DIM))
    moba_out_norm_w = gain((L, GROUP_WIDTH))
    ssd_conv_w = nrm((L, SSD_CONV, 1, SSD_XBC), SSD_CONV ** -0.5)
    ssd_conv_b = nrm((L, SSD_XBC), 0.01)
    dt0 = jnp.exp(jax.random.uniform(next(keys), (L, SSD_HEADS), F32, math.log(1e-3), math.log(1e-1)))
    ssd_dt_bias = dt0 + jnp.log(-jnp.expm1(-dt0))
    ssd_a_log = jnp.log(jax.random.uniform(next(keys), (L, SSD_HEADS), F32, 1.0, 16.0))
    ssd_d = gain((L, SSD_HEADS))
    ssd_norm_w = gain((L, GROUP_WIDTH))
    w_out = nrm((L, MIX_WIDTH, D_MODEL), MIX_WIDTH ** -0.5)
    mlp_norm_w = gain((L, D_MODEL))
    w_up = nrm((L, D_MODEL, D_FF), D_MODEL ** -0.5)
    w_down = nrm((L, D_FF, D_MODEL), D_FF ** -0.5)
    final_norm_w = gain((D_MODEL,))
    return {'x': x, 'rel_bias_table': rel_bias_table, 'attn_norm_w': attn_norm_w, 'w_in': w_in,
            's5_lam_re': s5_lam_re, 's5_lam_im': s5_lam_im, 's5_log_dt': s5_log_dt,
            's5_b_re': s5_b_re, 's5_b_im': s5_b_im, 's5_c_re': s5_c_re, 's5_c_im': s5_c_im,
            's5_d': s5_d, 's5_w_glu': s5_w_glu, 's5_out_norm_w': s5_out_norm_w,
            'diff_lam_q1': diff_lam_q1, 'diff_lam_k1': diff_lam_k1, 'diff_lam_q2': diff_lam_q2,
            'diff_lam_k2': diff_lam_k2, 'diff_subln_w': diff_subln_w, 'moba_out_norm_w': moba_out_norm_w,
            'ssd_conv_w': ssd_conv_w, 'ssd_conv_b': ssd_conv_b, 'ssd_dt_bias': ssd_dt_bias,
            'ssd_a_log': ssd_a_log, 'ssd_d': ssd_d, 'ssd_norm_w': ssd_norm_w, 'w_out': w_out,
            'mlp_norm_w': mlp_norm_w, 'w_up': w_up, 'w_down': w_down, 'final_norm_w': final_norm_w}


def reference(x, rel_bias_table, attn_norm_w, w_in, s5_lam_re, s5_lam_im, s5_log_dt, s5_b_re, s5_b_im,
              s5_c_re, s5_c_im, s5_d, s5_w_glu, s5_out_norm_w, diff_lam_q1, diff_lam_k1, diff_lam_q2,
              diff_lam_k2, diff_subln_w, moba_out_norm_w, ssd_conv_w, ssd_conv_b, ssd_dt_bias, ssd_a_log,
              ssd_d, ssd_norm_w, w_out, mlp_norm_w, w_up, w_down, final_norm_w):
    for l in range(DEPTH):
        h = rms_norm(x, attn_norm_w[l])
        proj = h @ w_in[l]
        u_s5 = proj[..., OFF_S5:OFF_DIFF]
        dq, dk, dv = jnp.split(proj[..., OFF_DIFF:OFF_MOBA], 3, axis=-1)
        mq, mk, mv = jnp.split(proj[..., OFF_MOBA:OFF_SSD], 3, axis=-1)
        ssd_in = proj[..., OFF_SSD:]
        z = ssd_in[..., :GROUP_WIDTH]
        xbc = ssd_in[..., GROUP_WIDTH:GROUP_WIDTH + SSD_XBC]
        dt_raw = ssd_in[..., GROUP_WIDTH + SSD_XBC:]

        y_s5 = rms_norm(s5_mixer(u_s5, s5_lam_re[l], s5_lam_im[l], s5_log_dt[l], s5_b_re[l], s5_b_im[l],
                                 s5_c_re[l], s5_c_im[l], s5_d[l], s5_w_glu[l]), s5_out_norm_w[l])
        y_diff = diff_attention(dq, dk, dv, diff_lam_q1[l], diff_lam_k1[l], diff_lam_q2[l], diff_lam_k2[l],
                                diff_subln_w[l], rel_bias_table, l)
        y_moba = rms_norm(moba_attention(mq, mk, mv, rel_bias_table), moba_out_norm_w[l])
        y_ssd = ssd_mixer(z, xbc, dt_raw, ssd_conv_w[l], ssd_conv_b[l], ssd_dt_bias[l], ssd_a_log[l],
                          ssd_d[l], ssd_norm_w[l])

        mixed = jnp.concatenate([y_s5, y_diff, y_moba, y_ssd], axis=-1)
        x = x + mixed @ w_out[l]
        h = rms_norm(x, mlp_norm_w[l])
        x = x + jnp.square(jax.nn.relu(h @ w_up[l])) @ w_down[l]
    return rms_norm(x, final_norm_w)
```

```python
import functools
import math

import jax
import jax.numpy as jnp
from jax import lax
from jax.experimental import pallas as pl
from jax.experimental.pallas import tpu as pltpu

F32 = jnp.float32
BF16 = jnp.bfloat16
EPS = 1e-6
NEG_INF = -1e30

GROUP_WIDTH = 1024
S5_CH_PER_GROUP = 16
S5_STATE = 64
DIFF_HEADS = 8
DIFF_HEAD_DIM = 64
MOBA_HEADS = 8
MOBA_BLOCK = 256
MOBA_TOPK = 3
SSD_HEAD_DIM = 64
SSD_HEADS = 16
SSD_GROUPS = 4
SSD_STATE = 128
SSD_CONV = 4
SSD_BC = SSD_GROUPS * SSD_STATE
SSD_XBC = GROUP_WIDTH + 2 * SSD_BC
REL_BUCKETS = 32
REL_MAX_DIST = 128

COL_S5 = 0
COL_DQ, COL_DK, COL_DV = 1, 2, 3
COL_MQ, COL_MK, COL_MV = 4, 5, 6
COL_Z = 7
COL_XBC = 4
PROJ_MAIN = 10 * GROUP_WIDTH

LANES = 128
ATT_BLOCK = 256
S5_LANE_CHUNK = 512
S5_CHUNKS = (GROUP_WIDTH // S5_CH_PER_GROUP) * S5_STATE // S5_LANE_CHUNK
VMEM_LIMIT = 56 * 1024 * 1024


def _cparams(sem):
    return pltpu.CompilerParams(dimension_semantics=sem, vmem_limit_bytes=VMEM_LIMIT)


def _rmsnorm_kernel(x_ref, w_ref, o_ref):
    x = x_ref[...].astype(F32)
    ms = jnp.mean(x * x, axis=-1, keepdims=True)
    o_ref[...] = (x * lax.rsqrt(ms + EPS) * w_ref[...]).astype(o_ref.dtype)


def rmsnorm(x, w, out_dtype, tm=256):
    t, d = x.shape
    return pl.pallas_call(
        _rmsnorm_kernel,
        grid=(t // tm,),
        in_specs=[pl.BlockSpec((tm, d), lambda i: (i, 0)), pl.BlockSpec((1, d), lambda i: (0, 0))],
        out_specs=pl.BlockSpec((tm, d), lambda i: (i, 0)),
        out_shape=jax.ShapeDtypeStruct((t, d), out_dtype),
        compiler_params=_cparams(("parallel",)),
        name="rmsnorm",
    )(x, w.reshape(1, d).astype(F32))


def _mm_kernel(*refs, nk, act, n_lhs, has_res):
    a_refs = refs[:n_lhs]
    w_ref = refs[n_lhs]
    res_ref = refs[n_lhs + 1] if has_res else None
    o_ref = refs[n_lhs + 1 + int(has_res)]
    acc_ref = refs[n_lhs + 2 + int(has_res)]
    k = pl.program_id(2)

    def partial_dot():
        if n_lhs == 1:
            return jnp.dot(a_refs[0][...], w_ref[...], preferred_element_type=F32)
        kw = a_refs[0].shape[1]
        tot = None
        for i, a_ref in enumerate(a_refs):
            p = jnp.dot(a_ref[...], w_ref[i * kw:(i + 1) * kw, :], preferred_element_type=F32)
            tot = p if tot is None else tot + p
        return tot

    def finish(acc):
        if act == "relu2":
            r = jnp.maximum(acc, 0.0)
            acc = r * r
        if has_res:
            acc = res_ref[...] + acc
        o_ref[...] = acc.astype(o_ref.dtype)

    if nk == 1:
        finish(partial_dot())
        return

    @pl.when(k == 0)
    def _():
        acc_ref[...] = partial_dot()

    @pl.when(jnp.logical_and(k > 0, k < nk - 1))
    def _():
        acc_ref[...] += partial_dot()

    @pl.when(k == nk - 1)
    def _():
        finish(acc_ref[...] + partial_dot())


def matmul(lhs, w3, layer, n_out, out_dtype, act=None, residual=None, tm=1024, tn=1024, tk=2048):
    lhs = list(lhs)
    m = lhs[0].shape[0]
    kdim = w3.shape[1]
    tm, tn = min(tm, m), min(tn, n_out)
    if len(lhs) > 1:
        assert all(a.shape[1] * len(lhs) == kdim for a in lhs)
        tk = kdim
    tk = min(tk, kdim)
    nk = kdim // tk
    assert m % tm == 0 and n_out % tn == 0 and kdim % tk == 0
    in_specs = []
    for a in lhs:
        if len(lhs) == 1:
            in_specs.append(pl.BlockSpec((tm, tk), lambda i, j, k: (i, k)))
        else:
            in_specs.append(pl.BlockSpec((tm, a.shape[1]), lambda i, j, k: (i, 0)))
    in_specs.append(pl.BlockSpec((None, tk, tn), lambda i, j, k: (layer, k, j)))
    args = lhs + [w3]
    if residual is not None:
        in_specs.append(pl.BlockSpec((tm, tn), lambda i, j, k: (i, j)))
        args.append(residual)
    return pl.pallas_call(
        functools.partial(_mm_kernel, nk=nk, act=act, n_lhs=len(lhs), has_res=residual is not None),
        grid=(m // tm, n_out // tn, nk),
        in_specs=in_specs,
        out_specs=pl.BlockSpec((tm, tn), lambda i, j, k: (i, j)),
        out_shape=jax.ShapeDtypeStruct((m, n_out), out_dtype),
        scratch_shapes=[pltpu.VMEM((tm, tn) if nk > 1 else (8, LANES), F32)],
        compiler_params=_cparams(("parallel", "parallel", "arbitrary")),
        name="matmul_" + (act or "lin") + ("_res" if residual is not None else ""),
    )(*args)


def _split3(x):
    hi = x.astype(BF16)
    r1 = x - hi.astype(F32)
    mid = r1.astype(BF16)
    lo = (r1 - mid.astype(F32)).astype(BF16)
    return hi, mid, lo


def _dot_exact_lhs(x, sel):
    hi, mid, lo = _split3(x)
    d = functools.partial(jnp.dot, preferred_element_type=F32)
    return d(hi, sel) + d(mid, sel) + d(lo, sel)


def _dot_exact_rhs(sel, x):
    hi, mid, lo = _split3(x)
    d = functools.partial(jnp.dot, preferred_element_type=F32)
    return d(sel, hi) + d(sel, mid) + d(sel, lo)


def _gelu_tanh(x):
    c = math.sqrt(2.0 / math.pi)
    return 0.5 * x * (1.0 + jnp.tanh(c * (x + 0.044715 * (x * x * x))))


def _s5_kernel(u_ref, bre_ref, bim_ref, cre_ref, cim_ref, are_ref, aim_ref, d_ref, wglu_ref, nw_ref,
               o_ref, sr_ref, si_ref, y_ref, st_re, st_im, *, tc):
    @pl.when(pl.program_id(1) == 0)
    def _():
        st_re[...] = jnp.zeros_like(st_re)
        st_im[...] = jnp.zeros_like(st_im)

    cw = S5_LANE_CHUNK
    uw = cw // S5_STATE * S5_CH_PER_GROUP
    for j in range(S5_CHUNKS):
        uj = u_ref[:, j * uw:(j + 1) * uw]
        sr_ref[...] = jnp.dot(uj, bre_ref[j], preferred_element_type=F32)
        si_ref[...] = jnp.dot(uj, bim_ref[j], preferred_element_type=F32)
        ar = are_ref[:, j * cw:(j + 1) * cw]
        ai = aim_ref[:, j * cw:(j + 1) * cw]

        def step(t, carry, ar=ar, ai=ai):
            pr, pi = carry
            nr = ar * pr - ai * pi + sr_ref[pl.ds(t, 1), :]
            ni = ar * pi + ai * pr + si_ref[pl.ds(t, 1), :]
            sr_ref[pl.ds(t, 1), :] = nr
            si_ref[pl.ds(t, 1), :] = ni
            return nr, ni

        pr, pi = lax.fori_loop(0, tc, step, (st_re[:, j * cw:(j + 1) * cw], st_im[:, j * cw:(j + 1) * cw]),
                               unroll=8)
        st_re[:, j * cw:(j + 1) * cw] = pr
        st_im[:, j * cw:(j + 1) * cw] = pi
        y_ref[:, j * uw:(j + 1) * uw] = (
            jnp.dot(sr_ref[...].astype(BF16), cre_ref[j], preferred_element_type=F32)
            - jnp.dot(si_ref[...].astype(BF16), cim_ref[j], preferred_element_type=F32))

    y = _gelu_tanh(y_ref[...] + d_ref[...] * u_ref[...].astype(F32))
    gate = jnp.dot(y.astype(BF16), wglu_ref[...], preferred_element_type=F32)
    out = y * jax.nn.sigmoid(gate)
    ms = jnp.mean(out * out, axis=-1, keepdims=True)
    o_ref[...] = (out * lax.rsqrt(ms + EPS) * nw_ref[...]).astype(o_ref.dtype)


def _s5_discretise(lam_re, lam_im, log_dt, b_re, b_im):
    dt = jnp.exp(log_dt.astype(F32))[:, None]
    lr, li = lam_re.astype(F32), lam_im.astype(F32)
    mag = jnp.exp(lr * dt)
    ab_re = mag * jnp.cos(li * dt)
    ab_im = mag * jnp.sin(li * dt)
    den = lr * lr + li * li
    f_re = ((ab_re - 1.0) * lr + ab_im * li) / den
    f_im = (ab_im * lr - (ab_re - 1.0) * li) / den
    br, bi = b_re.astype(F32), b_im.astype(F32)
    bb_re = f_re[..., None] * br - f_im[..., None] * bi
    bb_im = f_re[..., None] * bi + f_im[..., None] * br
    return ab_re, ab_im, bb_re, bb_im


def _block_diag(blocks, per):
    g, r, c = blocks.shape
    b = blocks.reshape(g // per, per, r, c)
    eye = jnp.eye(per, dtype=blocks.dtype)
    return jnp.einsum("nirc,ij->nirjc", b, eye).reshape(g // per, per * r, per * c)


def s5_mixer(proj, bsz, seq, lam_re, lam_im, log_dt, b_re, b_im, c_re, c_im, d_skip, w_glu, norm_w, tc=256):
    per = S5_LANE_CHUNK // S5_STATE
    ab_re, ab_im, bb_re, bb_im = _s5_discretise(lam_re, lam_im, log_dt, b_re, b_im)
    bre = _block_diag(jnp.swapaxes(bb_re, 1, 2), per).astype(BF16)
    bim = _block_diag(jnp.swapaxes(bb_im, 1, 2), per).astype(BF16)
    cre = _block_diag(jnp.swapaxes(c_re.astype(F32), 1, 2), per).astype(BF16)
    cim = _block_diag(jnp.swapaxes(c_im.astype(F32), 1, 2), per).astype(BF16)
    nstate = ab_re.size
    tc = min(tc, seq)
    nct = seq // tc
    uw = per * S5_CH_PER_GROUP
    const2 = lambda b, c: (0, 0)
    const3 = lambda b, c: (0, 0, 0)
    return pl.pallas_call(
        functools.partial(_s5_kernel, tc=tc),
        grid=(bsz, nct),
        in_specs=[
            pl.BlockSpec((tc, GROUP_WIDTH), lambda b, c: (b * nct + c, COL_S5)),
            pl.BlockSpec((S5_CHUNKS, uw, S5_LANE_CHUNK), const3),
            pl.BlockSpec((S5_CHUNKS, uw, S5_LANE_CHUNK), const3),
            pl.BlockSpec((S5_CHUNKS, S5_LANE_CHUNK, uw), const3),
            pl.BlockSpec((S5_CHUNKS, S5_LANE_CHUNK, uw), const3),
            pl.BlockSpec((1, nstate), const2),
            pl.BlockSpec((1, nstate), const2),
            pl.BlockSpec((1, GROUP_WIDTH), const2),
            pl.BlockSpec((GROUP_WIDTH, GROUP_WIDTH), const2),
            pl.BlockSpec((1, GROUP_WIDTH), const2),
        ],
        out_specs=pl.BlockSpec((tc, GROUP_WIDTH), lambda b, c: (b * nct + c, 0)),
        out_shape=jax.ShapeDtypeStruct((bsz * seq, GROUP_WIDTH), BF16),
        scratch_shapes=[
            pltpu.VMEM((tc, S5_LANE_CHUNK), F32),
            pltpu.VMEM((tc, S5_LANE_CHUNK), F32),
            pltpu.VMEM((tc, GROUP_WIDTH), F32),
            pltpu.VMEM((1, nstate), F32),
            pltpu.VMEM((1, nstate), F32),
        ],
        compiler_params=_cparams(("parallel", "arbitrary")),
        name="s5_mixer",
    )(proj, bre, bim, cre, cim, ab_re.reshape(1, nstate), ab_im.reshape(1, nstate),
      d_skip.reshape(1, GROUP_WIDTH).astype(F32), w_glu.astype(BF16), norm_w.reshape(1, GROUP_WIDTH).astype(F32))


def _rel_bucket(dist):
    n = jnp.maximum(dist, 0)
    max_exact = REL_BUCKETS // 2
    log_ratio = jnp.log(jnp.maximum(n, 1).astype(F32) / max_exact) / math.log(REL_MAX_DIST / max_exact)
    large = max_exact + (log_ratio * (REL_BUCKETS - max_exact)).astype(jnp.int32)
    large = jnp.minimum(large, REL_BUCKETS - 1)
    return jnp.where(n < max_exact, n, large)


def _bias_tiles(tbl, blk):
    assert blk >= REL_MAX_DIST
    i = jnp.arange(blk)[:, None]
    j = jnp.arange(blk)[None, :]
    t0 = jnp.where(i >= j, tbl[:, _rel_bucket(i - j)], NEG_INF)
    t1 = tbl[:, _rel_bucket(blk + i - j)]
    t2 = tbl[:, _rel_bucket(jnp.full((blk, blk), 2 * blk))]
    return jnp.stack([t0, t1, t2], axis=1).astype(F32)


def _online_softmax_step(s, v, m_ref, l_ref, acc_ref):
    m_prev = m_ref[...]
    m_new = jnp.maximum(m_prev, jnp.max(s, axis=-1, keepdims=True))
    alpha = jnp.exp(m_prev - m_new)
    p = jnp.exp(s - m_new)
    l_ref[...] = alpha * l_ref[...] + jnp.sum(p, axis=-1, keepdims=True)
    acc_ref[...] = alpha * acc_ref[...] + jnp.dot(p.astype(BF16), v, preferred_element_type=F32)
    m_ref[...] = m_new


def _qk(q, k):
    return lax.dot_general(q, k, (((1,), (1,)), ((), ())), preferred_element_type=F32)


def _attn_kernel(q_ref, k_ref, v_ref, bias_ref, p0_ref, p1_ref, o_ref, m_ref, l_ref, acc_ref, sel_ref, out_ref,
                 *, mode, scale, heads):
    blk = ATT_BLOCK
    own = pl.program_id(1)
    lane = lax.broadcasted_iota(jnp.int32, (blk, LANES), 1)
    n_maps = 2 if mode == "diff" else 1

    for h in range(heads):
        hs = slice(h * LANES, (h + 1) * LANES)
        q = q_ref[:, hs]
        if mode == "diff":
            qf = q.astype(F32)
            qs = [jnp.where(lane < DIFF_HEAD_DIM, qf, 0.0).astype(BF16),
                  jnp.where(lane >= DIFF_HEAD_DIM, qf, 0.0).astype(BF16)]
        else:
            qs = [q]
            nb = k_ref.shape[0] // blk
            kf = k_ref[:, hs].astype(F32).reshape(nb, blk, LANES)
            kmean = jnp.sum(kf, axis=1) * (1.0 / blk)
            kmean = jnp.concatenate([kmean, jnp.zeros((LANES - nb, LANES), F32)], axis=0)
            hi, mid, lo = _split3(kmean)
            gate = _qk(q, hi) + _qk(q, mid) + _qk(q, lo)
            gate = jnp.where(lane < own, gate, NEG_INF)
            for n in range(nb):
                col = gate[:, n:n + 1]
                beats = jnp.logical_and(
                    jnp.logical_or(gate > col, jnp.logical_and(gate == col, lane < n)), lane < nb)
                rank = jnp.sum(beats.astype(F32), axis=-1, keepdims=True)
                keep = jnp.logical_and(rank < MOBA_TOPK, n < own)
                sel_ref[n] = jnp.broadcast_to(jnp.where(keep, 0.0, NEG_INF), (blk, LANES))

        k0 = k_ref[pl.ds(pl.multiple_of(own * blk, blk), blk), hs]
        v0 = v_ref[pl.ds(pl.multiple_of(own * blk, blk), blk), hs]
        b0 = bias_ref[h, 0]
        for mi in range(n_maps):
            s = _qk(qs[mi], k0) * scale + b0
            m = jnp.max(s, axis=-1, keepdims=True)
            p = jnp.exp(s - m)
            m_ref[mi] = m
            l_ref[mi] = jnp.sum(p, axis=-1, keepdims=True)
            acc_ref[mi] = jnp.dot(p.astype(BF16), v0, preferred_element_type=F32)

        def past_tile(i, carry, h=h, hs=hs, qs=qs):
            n = own - 1 - i
            start = pl.multiple_of(n * blk, blk)
            kn = k_ref[pl.ds(start, blk), hs]
            vn = v_ref[pl.ds(start, blk), hs]
            bn = bias_ref[h, jnp.minimum(i + 1, 2)]
            if mode == "moba":
                sm = sel_ref[n]
                bn = bn + jnp.concatenate([sm] * (blk // LANES), axis=1)
            for mi in range(n_maps):
                s = _qk(qs[mi], kn) * scale + bn
                _online_softmax_step(s, vn, m_ref.at[mi], l_ref.at[mi], acc_ref.at[mi])
            return carry

        lax.fori_loop(0, own, past_tile, 0)

        if mode == "diff":
            o = acc_ref[0] / l_ref[0] - p0_ref[...] * (acc_ref[1] / l_ref[1])
            ms = jnp.mean(o * o, axis=-1, keepdims=True)
            o_ref[:, hs] = (o * lax.rsqrt(ms + EPS) * p1_ref[...]).astype(o_ref.dtype)
        else:
            out_ref[:, hs] = acc_ref[0] / l_ref[0]

    if mode == "moba":
        o = out_ref[...]
        ms = jnp.mean(o * o, axis=-1, keepdims=True)
        o_ref[...] = (o * lax.rsqrt(ms + EPS) * p1_ref[...]).astype(o_ref.dtype)


def attention(proj, bsz, seq, col_q, col_k, col_v, bias, p0, p1, mode, scale, heads):
    blk = ATT_BLOCK
    nq = seq // blk
    width = heads * LANES
    assert width == GROUP_WIDTH and seq % blk == 0
    nb = seq // blk
    assert nb <= LANES
    const2 = lambda b, i: (0, 0)
    return pl.pallas_call(
        functools.partial(_attn_kernel, mode=mode, scale=scale, heads=heads),
        grid=(bsz, nq),
        in_specs=[
            pl.BlockSpec((blk, width), lambda b, i: (b * nq + i, col_q)),
            pl.BlockSpec((seq, width), lambda b, i: (b, col_k)),
            pl.BlockSpec((seq, width), lambda b, i: (b, col_v)),
            pl.BlockSpec((heads, 3, blk, blk), lambda b, i: (0, 0, 0, 0)),
            pl.BlockSpec(p0.shape, const2),
            pl.BlockSpec(p1.shape, const2),
        ],
        out_specs=pl.BlockSpec((blk, width), lambda b, i: (b * nq + i, 0)),
        out_shape=jax.ShapeDtypeStruct((bsz * seq, width), BF16),
        scratch_shapes=[
            pltpu.VMEM((2, blk, 1), F32),
            pltpu.VMEM((2, blk, 1), F32),
            pltpu.VMEM((2, blk, LANES), F32),
            pltpu.VMEM((nb, blk, LANES), F32),
            pltpu.VMEM((blk, width), F32),
        ],
        compiler_params=_cparams(("parallel", "arbitrary")),
        name="attn_" + mode,
    )(proj, proj, proj, bias, p0, p1)


def _softplus(x):
    return jnp.maximum(x, 0.0) + jnp.log(1.0 + jnp.exp(-jnp.abs(x)))


def _silu(x):
    return x * jax.nn.sigmoid(x)


def _ssd_kernel(z_ref, xbc_ref, dt_ref, dtt_ref, cw_ref, cb_ref, dtb_ref, dtbt_ref, a_ref, at_ref, dsk_ref,
                nw_ref, o_ref, ext_ref, st_ref, y_ref, *, lc):
    halo = 8
    c = pl.program_id(1)

    @pl.when(c == 0)
    def _():
        ext_ref[0:halo, :] = jnp.zeros((halo, SSD_XBC), F32)
        st_ref[...] = jnp.zeros_like(st_ref)

    ext_ref[halo:halo + lc, :] = xbc_ref[...].astype(F32)
    conv = cb_ref[...] + cw_ref[SSD_CONV - 1:SSD_CONV, :] * ext_ref[halo:halo + lc, :]
    for kk in range(1, SSD_CONV):
        conv = conv + cw_ref[SSD_CONV - 1 - kk:SSD_CONV - kk, :] * ext_ref[halo - kk:halo - kk + lc, :]
    ext_ref[0:halo, :] = ext_ref[lc:lc + halo, :]
    xbc = _silu(conv)
    xs = xbc[:, :GROUP_WIDTH]

    dt_c = _softplus(dt_ref[...] + dtb_ref[...])
    dt_r = _softplus(dtt_ref[...] + dtbt_ref[...])
    a_c = dt_c * a_ref[...]
    a_r = dt_r * at_ref[...]
    row = lax.broadcasted_iota(jnp.int32, (lc, lc), 0)
    colm = lax.broadcasted_iota(jnp.int32, (lc, lc), 1)
    causal = row >= colm
    tri = causal.astype(BF16)
    tri_t = (colm >= row).astype(BF16)
    acum_c = _dot_exact_rhs(tri, a_c)
    acum_r = _dot_exact_lhs(a_r, tri_t)
    alast_c = acum_c[lc - 1:lc, :]

    lane = lax.broadcasted_iota(jnp.int32, (lc, LANES), 1)
    lo_half = lane < SSD_HEAD_DIM
    heads_per_group = SSD_HEADS // SSD_GROUPS
    for g in range(SSD_GROUPS):
        bg = xbc[:, GROUP_WIDTH + g * SSD_STATE:GROUP_WIDTH + (g + 1) * SSD_STATE].astype(BF16)
        cg = xbc[:, GROUP_WIDTH + SSD_BC + g * SSD_STATE:GROUP_WIDTH + SSD_BC + (g + 1) * SSD_STATE].astype(BF16)
        cb = _qk(cg, bg)
        for pr in range(heads_per_group // 2):
            ha = g * heads_per_group + 2 * pr
            hb = ha + 1
            tile = slice(ha * SSD_HEAD_DIM, (ha + 2) * SSD_HEAD_DIM)
            xp = xs[:, tile]

            def per_head(col_a, col_b):
                return jnp.where(lo_half, col_a, col_b)

            xdt = xp * per_head(dt_c[:, ha:ha + 1], dt_c[:, hb:hb + 1])
            ydiag = None
            for hh, keep in ((ha, lo_half), (hb, jnp.logical_not(lo_half))):
                seg = acum_c[:, hh:hh + 1] - acum_r[hh:hh + 1, :]
                decay = jnp.where(causal, jnp.exp(jnp.where(causal, seg, 0.0)), 0.0)
                mm = (cb * decay).astype(BF16)
                part = jnp.dot(mm, jnp.where(keep, xdt, 0.0).astype(BF16), preferred_element_type=F32)
                ydiag = part if ydiag is None else ydiag + part
            st = st_ref[ha // 2]
            yoff = _qk(cg, st.astype(BF16)) * per_head(jnp.exp(acum_c[:, ha:ha + 1]), jnp.exp(acum_c[:, hb:hb + 1]))
            y_ref[:, tile] = ydiag + yoff + dsk_ref[:, tile] * xp
            to_end = per_head(jnp.exp(alast_c[:, ha:ha + 1] - acum_c[:, ha:ha + 1]),
                              jnp.exp(alast_c[:, hb:hb + 1] - acum_c[:, hb:hb + 1]))
            xdec_t = jnp.transpose(xdt * to_end).astype(BF16)
            sub = lax.broadcasted_iota(jnp.int32, (LANES, SSD_STATE), 0)
            chunk_decay = jnp.where(sub < SSD_HEAD_DIM, jnp.exp(alast_c[:, ha:ha + 1]),
                                    jnp.exp(alast_c[:, hb:hb + 1]))
            st_ref[ha // 2] = st * chunk_decay + jnp.dot(xdec_t, bg, preferred_element_type=F32)

    y = y_ref[...] * _silu(z_ref[...].astype(F32))
    gw = GROUP_WIDTH // SSD_GROUPS
    for g in range(SSD_GROUPS):
        yg = y[:, g * gw:(g + 1) * gw]
        ms = jnp.mean(yg * yg, axis=-1, keepdims=True)
        o_ref[:, g * gw:(g + 1) * gw] = (yg * lax.rsqrt(ms + EPS) * nw_ref[:, g * gw:(g + 1) * gw]).astype(o_ref.dtype)


def _pad_lanes(v):
    return jnp.pad(v.astype(F32), (0, LANES - v.shape[0])).reshape(1, LANES)


def ssd_mixer(proj, dt_raw, bsz, seq, conv_w, conv_b, dt_bias, a_log, d_skip, norm_w, lc=128):
    nc = seq // lc
    t = bsz * seq
    a = -jnp.exp(a_log.astype(F32))
    dt_t = jnp.transpose(dt_raw[:, :SSD_HEADS])
    col16 = lambda v: jnp.broadcast_to(v.astype(F32)[:, None], (SSD_HEADS, LANES))
    dskip = jnp.repeat(d_skip.astype(F32), SSD_HEAD_DIM).reshape(1, GROUP_WIDTH)
    const2 = lambda b, c: (0, 0)
    return pl.pallas_call(
        functools.partial(_ssd_kernel, lc=lc),
        grid=(bsz, nc),
        in_specs=[
            pl.BlockSpec((lc, GROUP_WIDTH), lambda b, c: (b * nc + c, COL_Z)),
            pl.BlockSpec((lc, SSD_XBC), lambda b, c: (b * nc + c, COL_XBC)),
            pl.BlockSpec((lc, LANES), lambda b, c: (b * nc + c, 0)),
            pl.BlockSpec((SSD_HEADS, lc), lambda b, c: (0, b * nc + c)),
            pl.BlockSpec((SSD_CONV, SSD_XBC), const2),
            pl.BlockSpec((1, SSD_XBC), const2),
            pl.BlockSpec((1, LANES), const2),
            pl.BlockSpec((SSD_HEADS, LANES), const2),
            pl.BlockSpec((1, LANES), const2),
            pl.BlockSpec((SSD_HEADS, LANES), const2),
            pl.BlockSpec((1, GROUP_WIDTH), const2),
            pl.BlockSpec((1, GROUP_WIDTH), const2),
        ],
        out_specs=pl.BlockSpec((lc, GROUP_WIDTH), lambda b, c: (b * nc + c, 0)),
        out_shape=jax.ShapeDtypeStruct((t, GROUP_WIDTH), BF16),
        scratch_shapes=[
            pltpu.VMEM((lc + 8, SSD_XBC), F32),
            pltpu.VMEM((SSD_HEADS // 2, 2 * SSD_HEAD_DIM, SSD_STATE), F32),
            pltpu.VMEM((lc, GROUP_WIDTH), F32),
        ],
        compiler_params=_cparams(("parallel", "arbitrary")),
        name="ssd_mixer",
    )(proj, proj, dt_raw, dt_t, conv_w.reshape(SSD_CONV, SSD_XBC).astype(F32),
      conv_b.reshape(1, SSD_XBC).astype(F32), _pad_lanes(dt_bias), col16(dt_bias), _pad_lanes(a), col16(a),
      dskip, norm_w.reshape(1, GROUP_WIDTH).astype(F32))


def kernel(x, rel_bias_table, attn_norm_w, w_in, s5_lam_re, s5_lam_im, s5_log_dt, s5_b_re, s5_b_im, s5_c_re, s5_c_im, s5_d, s5_w_glu, s5_out_norm_w, diff_lam_q1, diff_lam_k1, diff_lam_q2, diff_lam_k2, diff_subln_w, moba_out_norm_w, ssd_conv_w, ssd_conv_b, ssd_dt_bias, ssd_a_log, ssd_d, ssd_norm_w, w_out, mlp_norm_w, w_up, w_down, final_norm_w):
    bsz, seq, d_model = x.shape
    depth = w_in.shape[0]
    t = bsz * seq
    x = x.reshape(t, d_model).astype(F32)

    w_in_b = w_in.astype(BF16)
    w_out_b = w_out.astype(BF16)
    w_up_b = w_up.astype(BF16)
    w_down_b = w_down.astype(BF16)
    w_dt_b = jnp.pad(w_in[:, :, PROJ_MAIN:], ((0, 0), (0, 0), (0, LANES - SSD_HEADS))).astype(BF16)

    tbl = rel_bias_table.astype(F32).T
    bias_diff = _bias_tiles(tbl[:DIFF_HEADS], ATT_BLOCK)
    bias_moba = _bias_tiles(tbl[DIFF_HEADS:], ATT_BLOCK)

    for l in range(depth):
        h = rmsnorm(x, attn_norm_w[l], BF16)
        proj = matmul([h], w_in_b, l, PROJ_MAIN, BF16)
        dt_raw = matmul([h], w_dt_b, l, LANES, F32)

        y_s5 = s5_mixer(proj, bsz, seq, s5_lam_re[l], s5_lam_im[l], s5_log_dt[l], s5_b_re[l], s5_b_im[l],
                        s5_c_re[l], s5_c_im[l], s5_d[l], s5_w_glu[l], s5_out_norm_w[l])

        lam_init = 0.8 - 0.6 * math.exp(-0.3 * l)
        lam = (jnp.exp(jnp.sum(diff_lam_q1[l].astype(F32) * diff_lam_k1[l].astype(F32)))
               - jnp.exp(jnp.sum(diff_lam_q2[l].astype(F32) * diff_lam_k2[l].astype(F32))) + lam_init)
        y_diff = attention(proj, bsz, seq, COL_DQ, COL_DK, COL_DV, bias_diff,
                           jnp.broadcast_to(lam, (1, LANES)).astype(F32),
                           (diff_subln_w[l].astype(F32) * (1.0 - lam_init)).reshape(1, LANES),
                           "diff", DIFF_HEAD_DIM ** -0.5, DIFF_HEADS)
        y_moba = attention(proj, bsz, seq, COL_MQ, COL_MK, COL_MV, bias_moba,
                           jnp.zeros((1, LANES), F32),
                           moba_out_norm_w[l].astype(F32).reshape(1, GROUP_WIDTH),
                           "moba", (GROUP_WIDTH // MOBA_HEADS) ** -0.5, MOBA_HEADS)
        y_ssd = ssd_mixer(proj, dt_raw, bsz, seq, ssd_conv_w[l], ssd_conv_b[l], ssd_dt_bias[l], ssd_a_log[l],
                          ssd_d[l], ssd_norm_w[l])

        x = matmul([y_s5, y_diff, y_moba, y_ssd], w_out_b, l, d_model, F32, residual=x, tn=512)
        h = rmsnorm(x, mlp_norm_w[l], BF16)
        u = matmul([h], w_up_b, l, w_up.shape[2], BF16, act="relu2")
        x = matmul([u], w_down_b, l, d_model, F32, residual=x)

    return rmsnorm(x, final_norm_w, F32).reshape(bsz, seq, d_model)
```

```python
import functools
import math

import jax
import jax.numpy as jnp
from jax import lax
from jax.experimental import pallas as pl
from jax.experimental.pallas import tpu as pltpu

F32 = jnp.float32
BF16 = jnp.bfloat16
EPS = 1e-6
NEG_INF = -1e30

GROUP_WIDTH = 1024
S5_CH_PER_GROUP = 16
S5_STATE = 64
DIFF_HEADS = 8
DIFF_HEAD_DIM = 64
MOBA_HEADS = 8
MOBA_BLOCK = 256
MOBA_TOPK = 3
SSD_HEAD_DIM = 64
SSD_HEADS = 16
SSD_GROUPS = 4
SSD_STATE = 128
SSD_CONV = 4
SSD_BC = SSD_GROUPS * SSD_STATE
SSD_XBC = GROUP_WIDTH + 2 * SSD_BC
REL_BUCKETS = 32
REL_MAX_DIST = 128

COL_S5 = 0
COL_DQ, COL_DK, COL_DV = 1, 2, 3
COL_MQ, COL_MK, COL_MV = 4, 5, 6
COL_Z = 7
COL_XBC = 4
PROJ_MAIN = 10 * GROUP_WIDTH

LANES = 128
ATT_BLOCK = 256
ATT_MAPS_PER_PASS = 8
GATE_ROWS = 16
LOG2E = math.log2(math.e)
S5_LANE_CHUNK = 512
S5_CHUNKS = (GROUP_WIDTH // S5_CH_PER_GROUP) * S5_STATE // S5_LANE_CHUNK
VMEM_LIMIT = 56 * 1024 * 1024


def _cparams(sem):
    return pltpu.CompilerParams(dimension_semantics=sem, vmem_limit_bytes=VMEM_LIMIT)


def _rmsnorm_kernel(x_ref, w_ref, o_ref):
    x = x_ref[...].astype(F32)
    ms = jnp.mean(x * x, axis=-1, keepdims=True)
    o_ref[...] = (x * lax.rsqrt(ms + EPS) * w_ref[...]).astype(o_ref.dtype)


def rmsnorm(x, w, out_dtype, tm=256):
    t, d = x.shape
    return pl.pallas_call(
        _rmsnorm_kernel,
        grid=(t // tm,),
        in_specs=[pl.BlockSpec((tm, d), lambda i: (i, 0)), pl.BlockSpec((1, d), lambda i: (0, 0))],
        out_specs=pl.BlockSpec((tm, d), lambda i: (i, 0)),
        out_shape=jax.ShapeDtypeStruct((t, d), out_dtype),
        compiler_params=_cparams(("parallel",)),
        name="rmsnorm",
    )(x, w.reshape(1, d).astype(F32))


def _mm_kernel(*refs, nk, act, n_lhs, has_res):
    a_refs = refs[:n_lhs]
    w_ref = refs[n_lhs]
    res_ref = refs[n_lhs + 1] if has_res else None
    o_ref = refs[n_lhs + 1 + int(has_res)]
    acc_ref = refs[n_lhs + 2 + int(has_res)]
    k = pl.program_id(2)

    def partial_dot():
        if n_lhs == 1:
            return jnp.dot(a_refs[0][...], w_ref[...], preferred_element_type=F32)
        kw = a_refs[0].shape[1]
        tot = None
        for i, a_ref in enumerate(a_refs):
            p = jnp.dot(a_ref[...], w_ref[i * kw:(i + 1) * kw, :], preferred_element_type=F32)
            tot = p if tot is None else tot + p
        return tot

    def finish(acc):
        if act == "relu2":
            r = jnp.maximum(acc, 0.0)
            acc = r * r
        if has_res:
            acc = res_ref[...] + acc
        o_ref[...] = acc.astype(o_ref.dtype)

    if nk == 1:
        finish(partial_dot())
        return

    @pl.when(k == 0)
    def _():
        acc_ref[...] = partial_dot()

    @pl.when(jnp.logical_and(k > 0, k < nk - 1))
    def _():
        acc_ref[...] += partial_dot()

    @pl.when(k == nk - 1)
    def _():
        finish(acc_ref[...] + partial_dot())


def matmul(lhs, w3, layer, n_out, out_dtype, act=None, residual=None, tm=1024, tn=1024, tk=2048):
    lhs = list(lhs)
    m = lhs[0].shape[0]
    kdim = w3.shape[1]
    tm, tn = min(tm, m), min(tn, n_out)
    if len(lhs) > 1:
        assert all(a.shape[1] * len(lhs) == kdim for a in lhs)
        tk = kdim
    tk = min(tk, kdim)
    nk = kdim // tk
    assert m % tm == 0 and n_out % tn == 0 and kdim % tk == 0
    in_specs = []
    for a in lhs:
        if len(lhs) == 1:
            in_specs.append(pl.BlockSpec((tm, tk), lambda i, j, k: (i, k)))
        else:
            in_specs.append(pl.BlockSpec((tm, a.shape[1]), lambda i, j, k: (i, 0)))
    in_specs.append(pl.BlockSpec((None, tk, tn), lambda i, j, k: (layer, k, j)))
    args = lhs + [w3]
    if residual is not None:
        in_specs.append(pl.BlockSpec((tm, tn), lambda i, j, k: (i, j)))
        args.append(residual)
    return pl.pallas_call(
        functools.partial(_mm_kernel, nk=nk, act=act, n_lhs=len(lhs), has_res=residual is not None),
        grid=(m // tm, n_out // tn, nk),
        in_specs=in_specs,
        out_specs=pl.BlockSpec((tm, tn), lambda i, j, k: (i, j)),
        out_shape=jax.ShapeDtypeStruct((m, n_out), out_dtype),
        scratch_shapes=[pltpu.VMEM((tm, tn) if nk > 1 else (8, LANES), F32)],
        compiler_params=_cparams(("parallel", "parallel", "arbitrary")),
        name="matmul_" + (act or "lin") + ("_res" if residual is not None else ""),
    )(*args)


def _split3(x):
    hi = x.astype(BF16)
    r1 = x - hi.astype(F32)
    mid = r1.astype(BF16)
    lo = (r1 - mid.astype(F32)).astype(BF16)
    return hi, mid, lo


def _dot_exact_lhs(x, sel):
    hi, mid, lo = _split3(x)
    d = functools.partial(jnp.dot, preferred_element_type=F32)
    return d(hi, sel) + d(mid, sel) + d(lo, sel)


def _dot_exact_rhs(sel, x):
    hi, mid, lo = _split3(x)
    d = functools.partial(jnp.dot, preferred_element_type=F32)
    return d(sel, hi) + d(sel, mid) + d(sel, lo)


def _gelu_tanh(x):
    c = math.sqrt(2.0 / math.pi)
    return 0.5 * x * (1.0 + jnp.tanh(c * (x + 0.044715 * (x * x * x))))


def _s5_kernel(u_ref, bre_ref, bim_ref, cre_ref, cim_ref, are_ref, aim_ref, d_ref, wglu_ref, nw_ref,
               o_ref, sr_ref, si_ref, y_ref, st_re, st_im, *, tc):
    @pl.when(pl.program_id(1) == 0)
    def _():
        st_re[...] = jnp.zeros_like(st_re)
        st_im[...] = jnp.zeros_like(st_im)

    cw = S5_LANE_CHUNK
    uw = cw // S5_STATE * S5_CH_PER_GROUP
    for j in range(S5_CHUNKS):
        uj = u_ref[:, j * uw:(j + 1) * uw]
        sr_ref[...] = jnp.dot(uj, bre_ref[j], preferred_element_type=F32)
        si_ref[...] = jnp.dot(uj, bim_ref[j], preferred_element_type=F32)
        ar = are_ref[:, j * cw:(j + 1) * cw]
        ai = aim_ref[:, j * cw:(j + 1) * cw]

        def step(t, carry, ar=ar, ai=ai):
            pr, pi = carry
            nr = ar * pr - ai * pi + sr_ref[pl.ds(t, 1), :]
            ni = ar * pi + ai * pr + si_ref[pl.ds(t, 1), :]
            sr_ref[pl.ds(t, 1), :] = nr
            si_ref[pl.ds(t, 1), :] = ni
            return nr, ni

        pr, pi = lax.fori_loop(0, tc, step, (st_re[:, j * cw:(j + 1) * cw], st_im[:, j * cw:(j + 1) * cw]),
                               unroll=8)
        st_re[:, j * cw:(j + 1) * cw] = pr
        st_im[:, j * cw:(j + 1) * cw] = pi
        y_ref[:, j * uw:(j + 1) * uw] = (
            jnp.dot(sr_ref[...].astype(BF16), cre_ref[j], preferred_element_type=F32)
            - jnp.dot(si_ref[...].astype(BF16), cim_ref[j], preferred_element_type=F32))

    y = _gelu_tanh(y_ref[...] + d_ref[...] * u_ref[...].astype(F32))
    gate = jnp.dot(y.astype(BF16), wglu_ref[...], preferred_element_type=F32)
    out = y * jax.nn.sigmoid(gate)
    ms = jnp.mean(out * out, axis=-1, keepdims=True)
    o_ref[...] = (out * lax.rsqrt(ms + EPS) * nw_ref[...]).astype(o_ref.dtype)


def _s5_discretise(lam_re, lam_im, log_dt, b_re, b_im):
    dt = jnp.exp(log_dt.astype(F32))[:, None]
    lr, li = lam_re.astype(F32), lam_im.astype(F32)
    mag = jnp.exp(lr * dt)
    ab_re = mag * jnp.cos(li * dt)
    ab_im = mag * jnp.sin(li * dt)
    den = lr * lr + li * li
    f_re = ((ab_re - 1.0) * lr + ab_im * li) / den
    f_im = (ab_im * lr - (ab_re - 1.0) * li) / den
    br, bi = b_re.astype(F32), b_im.astype(F32)
    bb_re = f_re[..., None] * br - f_im[..., None] * bi
    bb_im = f_re[..., None] * bi + f_im[..., None] * br
    return ab_re, ab_im, bb_re, bb_im


def _block_diag(blocks, per):
    g, r, c = blocks.shape
    b = blocks.reshape(g // per, per, r, c)
    eye = jnp.eye(per, dtype=blocks.dtype)
    return jnp.einsum("nirc,ij->nirjc", b, eye).reshape(g // per, per * r, per * c)


def s5_mixer(proj, bsz, seq, lam_re, lam_im, log_dt, b_re, b_im, c_re, c_im, d_skip, w_glu, norm_w, tc=256):
    per = S5_LANE_CHUNK // S5_STATE
    ab_re, ab_im, bb_re, bb_im = _s5_discretise(lam_re, lam_im, log_dt, b_re, b_im)
    bre = _block_diag(jnp.swapaxes(bb_re, 1, 2), per).astype(BF16)
    bim = _block_diag(jnp.swapaxes(bb_im, 1, 2), per).astype(BF16)
    cre = _block_diag(jnp.swapaxes(c_re.astype(F32), 1, 2), per).astype(BF16)
    cim = _block_diag(jnp.swapaxes(c_im.astype(F32), 1, 2), per).astype(BF16)
    nstate = ab_re.size
    tc = min(tc, seq)
    nct = seq // tc
    uw = per * S5_CH_PER_GROUP
    const2 = lambda b, c: (0, 0)
    const3 = lambda b, c: (0, 0, 0)
    return pl.pallas_call(
        functools.partial(_s5_kernel, tc=tc),
        grid=(bsz, nct),
        in_specs=[
            pl.BlockSpec((tc, GROUP_WIDTH), lambda b, c: (b * nct + c, COL_S5)),
            pl.BlockSpec((S5_CHUNKS, uw, S5_LANE_CHUNK), const3),
            pl.BlockSpec((S5_CHUNKS, uw, S5_LANE_CHUNK), const3),
            pl.BlockSpec((S5_CHUNKS, S5_LANE_CHUNK, uw), const3),
            pl.BlockSpec((S5_CHUNKS, S5_LANE_CHUNK, uw), const3),
            pl.BlockSpec((1, nstate), const2),
            pl.BlockSpec((1, nstate), const2),
            pl.BlockSpec((1, GROUP_WIDTH), const2),
            pl.BlockSpec((GROUP_WIDTH, GROUP_WIDTH), const2),
            pl.BlockSpec((1, GROUP_WIDTH), const2),
        ],
        out_specs=pl.BlockSpec((tc, GROUP_WIDTH), lambda b, c: (b * nct + c, 0)),
        out_shape=jax.ShapeDtypeStruct((bsz * seq, GROUP_WIDTH), BF16),
        scratch_shapes=[
            pltpu.VMEM((tc, S5_LANE_CHUNK), F32),
            pltpu.VMEM((tc, S5_LANE_CHUNK), F32),
            pltpu.VMEM((tc, GROUP_WIDTH), F32),
            pltpu.VMEM((1, nstate), F32),
            pltpu.VMEM((1, nstate), F32),
        ],
        compiler_params=_cparams(("parallel", "arbitrary")),
        name="s5_mixer",
    )(proj, bre, bim, cre, cim, ab_re.reshape(1, nstate), ab_im.reshape(1, nstate),
      d_skip.reshape(1, GROUP_WIDTH).astype(F32), w_glu.astype(BF16), norm_w.reshape(1, GROUP_WIDTH).astype(F32))


def _rel_bucket(dist):
    n = jnp.maximum(dist, 0)
    max_exact = REL_BUCKETS // 2
    log_ratio = jnp.log(jnp.maximum(n, 1).astype(F32) / max_exact) / math.log(REL_MAX_DIST / max_exact)
    large = max_exact + (log_ratio * (REL_BUCKETS - max_exact)).astype(jnp.int32)
    large = jnp.minimum(large, REL_BUCKETS - 1)
    return jnp.where(n < max_exact, n, large)


def _bias_tiles(tbl, blk):
    assert blk >= REL_MAX_DIST
    i = jnp.arange(blk)[:, None]
    j = jnp.arange(blk)[None, :]
    buckets = jnp.stack([_rel_bucket(i - j), _rel_bucket(blk + i - j), _rel_bucket(jnp.full((blk, blk), 2 * blk))])
    onehot = (buckets[..., None] == jnp.arange(REL_BUCKETS)).astype(F32)
    tiles = jnp.einsum("tijk,hk->htij", onehot, tbl, precision=lax.Precision.HIGHEST) * LOG2E
    visible = jnp.stack([i >= j, jnp.ones((blk, blk), bool), jnp.ones((blk, blk), bool)])
    return jnp.where(visible[None], tiles, NEG_INF).astype(F32)


def _qk(q, k):
    return lax.dot_general(q, k, (((1,), (1,)), ((), ())), preferred_element_type=F32)


def _attn_kernel(q_ref, k_ref, v_ref, bias_ref, p0_ref, p1_ref, o_ref,
                 qs_ref, s_ref, mx_ref, mb_ref, ls_ref, acc_ref, kmean_ref, out_ref, *, mode, scale, heads, group):
    blk = ATT_BLOCK
    own = pl.program_id(1)
    lane = lax.broadcasted_iota(jnp.int32, (blk, LANES), 1)
    n_maps = 2 if mode == "diff" else 1
    nb = k_ref.shape[0] // blk
    reps = blk // LANES

    if mode == "moba":
        @pl.when(own == 0)
        def _():
            for h in range(heads):
                kf = k_ref[:, h * LANES:(h + 1) * LANES].astype(F32).reshape(nb, blk, LANES)
                kmean = jnp.sum(kf, axis=1) * (1.0 / blk)
                kmean_ref[h] = jnp.concatenate([kmean, jnp.zeros((GATE_ROWS - nb, LANES), F32)], axis=0)

    for g0 in range(0, heads, group):
        maps = [(h, mi) for h in range(g0, g0 + group) for mi in range(n_maps)]
        for h in range(g0, g0 + group):
            hs = slice(h * LANES, (h + 1) * LANES)
            q = q_ref[:, hs]
            if mode == "diff":
                qf = q.astype(F32)
                qs_ref[(h - g0) * 2] = jnp.where(lane < DIFF_HEAD_DIM, qf, 0.0).astype(BF16)
                qs_ref[(h - g0) * 2 + 1] = jnp.where(lane >= DIFF_HEAD_DIM, qf, 0.0).astype(BF16)
            else:
                hi, mid, lo = _split3(kmean_ref[h])
                gate = _qk(hi, q) + _qk(mid, q) + _qk(lo, q)
                sub = lax.broadcasted_iota(jnp.int32, (GATE_ROWS, blk), 0)
                gate = jnp.where(sub < own, gate, NEG_INF)
                penalty = jnp.zeros((GATE_ROWS, blk), F32)
                for n in range(nb):
                    row = gate[n:n + 1, :]
                    beats = jnp.logical_and(
                        jnp.logical_or(gate > row, jnp.logical_and(gate == row, sub < n)), sub < nb)
                    rank = jnp.sum(beats.astype(F32), axis=0, keepdims=True)
                    keep = jnp.logical_or(jnp.logical_and(rank < MOBA_TOPK, n < own), n == own)
                    penalty = jnp.where(sub == n, jnp.where(keep, 0.0, NEG_INF), penalty)
                pen_t = jnp.transpose(
                    jnp.concatenate([penalty, jnp.zeros((LANES - GATE_ROWS, blk), F32)], axis=0))
                qs_ref[h - g0] = jnp.concatenate([q, pen_t.astype(BF16)], axis=1)

        for idx in range(len(maps)):
            mx_ref[idx] = jnp.full((blk, LANES), -jnp.inf, F32)

        def logits_pass(n, carry, g0=g0, maps=maps):
            start = pl.multiple_of(n * blk, blk)
            tile_dist = jnp.minimum(own - n, 2)
            if mode == "moba":
                onehot = jnp.where(lane == n, 1.0, 0.0).astype(BF16)
            for idx, (h, mi) in enumerate(maps):
                hs = slice(h * LANES, (h + 1) * LANES)
                kn = k_ref[pl.ds(start, blk), hs]
                if mode == "moba":
                    kn = jnp.concatenate([kn, onehot], axis=1)
                s = _qk(qs_ref[idx], kn) * (scale * LOG2E) + bias_ref[h, tile_dist]
                s_ref[idx, n] = s
                part = s[:, :LANES]
                for r in range(1, reps):
                    part = jnp.maximum(part, s[:, r * LANES:(r + 1) * LANES])
                mx_ref[idx] = jnp.maximum(mx_ref[idx], part)
            return carry

        lax.fori_loop(0, own + 1, logits_pass, 0)

        for idx in range(len(maps)):
            mb_ref[idx] = jnp.broadcast_to(jnp.max(mx_ref[idx], axis=-1, keepdims=True), (blk, LANES))
            ls_ref[idx] = jnp.zeros((blk, LANES), F32)
            acc_ref[idx] = jnp.zeros((blk, LANES), F32)

        def value_pass(n, carry, maps=maps):
            start = pl.multiple_of(n * blk, blk)
            for idx, (h, mi) in enumerate(maps):
                hs = slice(h * LANES, (h + 1) * LANES)
                s = s_ref[idx, n]
                mb = mb_ref[idx]
                ps = [jnp.exp2(s[:, r * LANES:(r + 1) * LANES] - mb) for r in range(reps)]
                tot = ps[0]
                for r in range(1, reps):
                    tot = tot + ps[r]
                ls_ref[idx] += tot
                p = jnp.concatenate(ps, axis=1).astype(BF16)
                acc_ref[idx] += jnp.dot(p, v_ref[pl.ds(start, blk), hs], preferred_element_type=F32)
            return carry

        lax.fori_loop(0, own + 1, value_pass, 0)

        for h in range(g0, g0 + group):
            hs = slice(h * LANES, (h + 1) * LANES)
            i0 = (h - g0) * n_maps
            o = acc_ref[i0] / jnp.sum(ls_ref[i0], axis=-1, keepdims=True)
            if mode == "diff":
                o2 = acc_ref[i0 + 1] / jnp.sum(ls_ref[i0 + 1], axis=-1, keepdims=True)
                o = o - p0_ref[...] * o2
                ms = jnp.mean(o * o, axis=-1, keepdims=True)
                o_ref[:, hs] = (o * lax.rsqrt(ms + EPS) * p1_ref[...]).astype(o_ref.dtype)
            else:
                out_ref[:, hs] = o

    if mode == "moba":
        o = out_ref[...]
        ms = jnp.mean(o * o, axis=-1, keepdims=True)
        o_ref[...] = (o * lax.rsqrt(ms + EPS) * p1_ref[...]).astype(o_ref.dtype)


def attention(proj, bsz, seq, col_q, col_k, col_v, bias, p0, p1, mode, scale, heads):
    blk = ATT_BLOCK
    nq = seq // blk
    width = heads * LANES
    assert width == GROUP_WIDTH and seq % blk == 0
    nb = seq // blk
    assert nb <= GATE_ROWS
    n_maps = 2 if mode == "diff" else 1
    group = ATT_MAPS_PER_PASS // n_maps
    const2 = lambda b, i: (0, 0)
    return pl.pallas_call(
        functools.partial(_attn_kernel, mode=mode, scale=scale, heads=heads, group=group),
        grid=(bsz, nq),
        in_specs=[
            pl.BlockSpec((blk, width), lambda b, i: (b * nq + i, col_q)),
            pl.BlockSpec((seq, width), lambda b, i: (b, col_k)),
            pl.BlockSpec((seq, width), lambda b, i: (b, col_v)),
            pl.BlockSpec((heads, 3, blk, blk), lambda b, i: (0, 0, 0, 0)),
            pl.BlockSpec(p0.shape, const2),
            pl.BlockSpec(p1.shape, const2),
        ],
        out_specs=pl.BlockSpec((blk, width), lambda b, i: (b * nq + i, 0)),
        out_shape=jax.ShapeDtypeStruct((bsz * seq, width), BF16),
        scratch_shapes=[
            pltpu.VMEM((ATT_MAPS_PER_PASS, blk, LANES * (1 if mode == "diff" else 2)), BF16),
            pltpu.VMEM((ATT_MAPS_PER_PASS, nb, blk, blk), F32),
            pltpu.VMEM((ATT_MAPS_PER_PASS, blk, LANES), F32),
            pltpu.VMEM((ATT_MAPS_PER_PASS, blk, LANES), F32),
            pltpu.VMEM((ATT_MAPS_PER_PASS, blk, LANES), F32),
            pltpu.VMEM((ATT_MAPS_PER_PASS, blk, LANES), F32),
            pltpu.VMEM((heads, GATE_ROWS, LANES), F32),
            pltpu.VMEM((blk, width), F32),
        ],
        compiler_params=_cparams(("parallel", "arbitrary")),
        name="attn_" + mode,
    )(proj, proj, proj, bias, p0, p1)


def _softplus(x):
    return jnp.maximum(x, 0.0) + jnp.log(1.0 + jnp.exp(-jnp.abs(x)))


def _silu(x):
    return x * jax.nn.sigmoid(x)


def _ssd_kernel(z_ref, xbc_ref, dt_ref, dtt_ref, cw_ref, cb_ref, dtb_ref, dtbt_ref, a_ref, at_ref, dsk_ref,
                nw_ref, o_ref, ext_ref, st_ref, y_ref, *, lc):
    halo = 8
    c = pl.program_id(1)

    @pl.when(c == 0)
    def _():
        ext_ref[0:halo, :] = jnp.zeros((halo, SSD_XBC), F32)
        st_ref[...] = jnp.zeros_like(st_ref)

    ext_ref[halo:halo + lc, :] = xbc_ref[...].astype(F32)
    conv = cb_ref[...] + cw_ref[SSD_CONV - 1:SSD_CONV, :] * ext_ref[halo:halo + lc, :]
    for kk in range(1, SSD_CONV):
        conv = conv + cw_ref[SSD_CONV - 1 - kk:SSD_CONV - kk, :] * ext_ref[halo - kk:halo - kk + lc, :]
    ext_ref[0:halo, :] = ext_ref[lc:lc + halo, :]
    xbc = _silu(conv)
    xs = xbc[:, :GROUP_WIDTH]

    dt_c = _softplus(dt_ref[...] + dtb_ref[...])
    dt_r = _softplus(dtt_ref[...] + dtbt_ref[...])
    a_c = dt_c * a_ref[...]
    a_r = dt_r * at_ref[...]
    row = lax.broadcasted_iota(jnp.int32, (lc, lc), 0)
    colm = lax.broadcasted_iota(jnp.int32, (lc, lc), 1)
    causal = row >= colm
    tri = causal.astype(BF16)
    tri_t = (colm >= row).astype(BF16)
    acum_c = _dot_exact_rhs(tri, a_c)
    acum_r = _dot_exact_lhs(a_r, tri_t)
    alast_c = acum_c[lc - 1:lc, :]

    lane = lax.broadcasted_iota(jnp.int32, (lc, LANES), 1)
    lo_half = lane < SSD_HEAD_DIM
    heads_per_group = SSD_HEADS // SSD_GROUPS
    for g in range(SSD_GROUPS):
        bg = xbc[:, GROUP_WIDTH + g * SSD_STATE:GROUP_WIDTH + (g + 1) * SSD_STATE].astype(BF16)
        cg = xbc[:, GROUP_WIDTH + SSD_BC + g * SSD_STATE:GROUP_WIDTH + SSD_BC + (g + 1) * SSD_STATE].astype(BF16)
        cb = _qk(cg, bg)
        for pr in range(heads_per_group // 2):
            ha = g * heads_per_group + 2 * pr
            hb = ha + 1
            tile = slice(ha * SSD_HEAD_DIM, (ha + 2) * SSD_HEAD_DIM)
            xp = xs[:, tile]

            def per_head(col_a, col_b):
                return jnp.where(lo_half, col_a, col_b)

            xdt = xp * per_head(dt_c[:, ha:ha + 1], dt_c[:, hb:hb + 1])
            ydiag = None
            for hh, keep in ((ha, lo_half), (hb, jnp.logical_not(lo_half))):
                seg = acum_c[:, hh:hh + 1] - acum_r[hh:hh + 1, :]
                decay = jnp.where(causal, jnp.exp(jnp.where(causal, seg, 0.0)), 0.0)
                mm = (cb * decay).astype(BF16)
                part = jnp.dot(mm, jnp.where(keep, xdt, 0.0).astype(BF16), preferred_element_type=F32)
                ydiag = part if ydiag is None else ydiag + part
            st = st_ref[ha // 2]
            yoff = _qk(cg, st.astype(BF16)) * per_head(jnp.exp(acum_c[:, ha:ha + 1]), jnp.exp(acum_c[:, hb:hb + 1]))
            y_ref[:, tile] = ydiag + yoff + dsk_ref[:, tile] * xp
            to_end = per_head(jnp.exp(alast_c[:, ha:ha + 1] - acum_c[:, ha:ha + 1]),
                              jnp.exp(alast_c[:, hb:hb + 1] - acum_c[:, hb:hb + 1]))
            xdec_t = jnp.transpose(xdt * to_end).astype(BF16)
            sub = lax.broadcasted_iota(jnp.int32, (LANES, SSD_STATE), 0)
            chunk_decay = jnp.where(sub < SSD_HEAD_DIM, jnp.exp(alast_c[:, ha:ha + 1]),
                                    jnp.exp(alast_c[:, hb:hb + 1]))
            st_ref[ha // 2] = st * chunk_decay + jnp.dot(xdec_t, bg, preferred_element_type=F32)

    y = y_ref[...] * _silu(z_ref[...].astype(F32))
    gw = GROUP_WIDTH // SSD_GROUPS
    for g in range(SSD_GROUPS):
        yg = y[:, g * gw:(g + 1) * gw]
        ms = jnp.mean(yg * yg, axis=-1, keepdims=True)
        o_ref[:, g * gw:(g + 1) * gw] = (yg * lax.rsqrt(ms + EPS) * nw_ref[:, g * gw:(g + 1) * gw]).astype(o_ref.dtype)


def _pad_lanes(v):
    return jnp.pad(v.astype(F32), (0, LANES - v.shape[0])).reshape(1, LANES)


def ssd_mixer(proj, dt_raw, bsz, seq, conv_w, conv_b, dt_bias, a_log, d_skip, norm_w, lc=128):
    nc = seq // lc
    t = bsz * seq
    a = -jnp.exp(a_log.astype(F32))
    dt_t = jnp.transpose(dt_raw[:, :SSD_HEADS])
    col16 = lambda v: jnp.broadcast_to(v.astype(F32)[:, None], (SSD_HEADS, LANES))
    dskip = jnp.repeat(d_skip.astype(F32), SSD_HEAD_DIM).reshape(1, GROUP_WIDTH)
    const2 = lambda b, c: (0, 0)
    return pl.pallas_call(
        functools.partial(_ssd_kernel, lc=lc),
        grid=(bsz, nc),
        in_specs=[
            pl.BlockSpec((lc, GROUP_WIDTH), lambda b, c: (b * nc + c, COL_Z)),
            pl.BlockSpec((lc, SSD_XBC), lambda b, c: (b * nc + c, COL_XBC)),
            pl.BlockSpec((lc, LANES), lambda b, c: (b * nc + c, 0)),
            pl.BlockSpec((SSD_HEADS, lc), lambda b, c: (0, b * nc + c)),
            pl.BlockSpec((SSD_CONV, SSD_XBC), const2),
            pl.BlockSpec((1, SSD_XBC), const2),
            pl.BlockSpec((1, LANES), const2),
            pl.BlockSpec((SSD_HEADS, LANES), const2),
            pl.BlockSpec((1, LANES), const2),
            pl.BlockSpec((SSD_HEADS, LANES), const2),
            pl.BlockSpec((1, GROUP_WIDTH), const2),
            pl.BlockSpec((1, GROUP_WIDTH), const2),
        ],
        out_specs=pl.BlockSpec((lc, GROUP_WIDTH), lambda b, c: (b * nc + c, 0)),
        out_shape=jax.ShapeDtypeStruct((t, GROUP_WIDTH), BF16),
        scratch_shapes=[
            pltpu.VMEM((lc + 8, SSD_XBC), F32),
            pltpu.VMEM((SSD_HEADS // 2, 2 * SSD_HEAD_DIM, SSD_STATE), F32),
            pltpu.VMEM((lc, GROUP_WIDTH), F32),
        ],
        compiler_params=_cparams(("parallel", "arbitrary")),
        name="ssd_mixer",
    )(proj, proj, dt_raw, dt_t, conv_w.reshape(SSD_CONV, SSD_XBC).astype(F32),
      conv_b.reshape(1, SSD_XBC).astype(F32), _pad_lanes(dt_bias), col16(dt_bias), _pad_lanes(a), col16(a),
      dskip, norm_w.reshape(1, GROUP_WIDTH).astype(F32))


def kernel(x, rel_bias_table, attn_norm_w, w_in, s5_lam_re, s5_lam_im, s5_log_dt, s5_b_re, s5_b_im, s5_c_re, s5_c_im, s5_d, s5_w_glu, s5_out_norm_w, diff_lam_q1, diff_lam_k1, diff_lam_q2, diff_lam_k2, diff_subln_w, moba_out_norm_w, ssd_conv_w, ssd_conv_b, ssd_dt_bias, ssd_a_log, ssd_d, ssd_norm_w, w_out, mlp_norm_w, w_up, w_down, final_norm_w):
    bsz, seq, d_model = x.shape
    depth = w_in.shape[0]
    t = bsz * seq
    x = x.reshape(t, d_model).astype(F32)

    w_in_b = w_in.astype(BF16)
    w_out_b = w_out.astype(BF16)
    w_up_b = w_up.astype(BF16)
    w_down_b = w_down.astype(BF16)
    w_dt_b = jnp.pad(w_in[:, :, PROJ_MAIN:], ((0, 0), (0, 0), (0, LANES - SSD_HEADS))).astype(BF16)

    tbl = rel_bias_table.astype(F32).T
    bias_diff = _bias_tiles(tbl[:DIFF_HEADS], ATT_BLOCK)
    bias_moba = _bias_tiles(tbl[DIFF_HEADS:], ATT_BLOCK)

    for l in range(depth):
        h = rmsnorm(x, attn_norm_w[l], BF16)
        proj = matmul([h], w_in_b, l, PROJ_MAIN, BF16)
        dt_raw = matmul([h], w_dt_b, l, LANES, F32)

        y_s5 = s5_mixer(proj, bsz, seq, s5_lam_re[l], s5_lam_im[l], s5_log_dt[l], s5_b_re[l], s5_b_im[l],
                        s5_c_re[l], s5_c_im[l], s5_d[l], s5_w_glu[l], s5_out_norm_w[l])

        lam_init = 0.8 - 0.6 * math.exp(-0.3 * l)
        lam = (jnp.exp(jnp.sum(diff_lam_q1[l].astype(F32) * diff_lam_k1[l].astype(F32)))
               - jnp.exp(jnp.sum(diff_lam_q2[l].astype(F32) * diff_lam_k2[l].astype(F32))) + lam_init)
        y_diff = attention(proj, bsz, seq, COL_DQ, COL_DK, COL_DV, bias_diff,
                           jnp.broadcast_to(lam, (1, LANES)).astype(F32),
                           (diff_subln_w[l].astype(F32) * (1.0 - lam_init)).reshape(1, LANES),
                           "diff", DIFF_HEAD_DIM ** -0.5, DIFF_HEADS)
        y_moba = attention(proj, bsz, seq, COL_MQ, COL_MK, COL_MV, bias_moba,
                           jnp.zeros((1, LANES), F32),
                           moba_out_norm_w[l].astype(F32).reshape(1, GROUP_WIDTH),
                           "moba", (GROUP_WIDTH // MOBA_HEADS) ** -0.5, MOBA_HEADS)
        y_ssd = ssd_mixer(proj, dt_raw, bsz, seq, ssd_conv_w[l], ssd_conv_b[l], ssd_dt_bias[l], ssd_a_log[l],
                          ssd_d[l], ssd_norm_w[l])

        x = matmul([y_s5, y_diff, y_moba, y_ssd], w_out_b, l, d_model, F32, residual=x, tn=512)
        h = rmsnorm(x, mlp_norm_w[l], BF16)
        u = matmul([h], w_up_b, l, w_up.shape[2], BF16, act="relu2")
        x = matmul([u], w_down_b, l, d_model, F32, residual=x)

    return rmsnorm(x, final_norm_w, F32).reshape(bsz, seq, d_model)
```

```python
import functools
import math

import jax
import jax.numpy as jnp
from jax import lax
from jax.experimental import pallas as pl
from jax.experimental.pallas import tpu as pltpu

F32 = jnp.float32
BF16 = jnp.bfloat16
EPS = 1e-6
NEG_INF = -1e30

GROUP_WIDTH = 1024
S5_CH_PER_GROUP = 16
S5_STATE = 64
DIFF_HEADS = 8
DIFF_HEAD_DIM = 64
MOBA_HEADS = 8
MOBA_BLOCK = 256
MOBA_TOPK = 3
SSD_HEAD_DIM = 64
SSD_HEADS = 16
SSD_GROUPS = 4
SSD_STATE = 128
SSD_CONV = 4
SSD_BC = SSD_GROUPS * SSD_STATE
SSD_XBC = GROUP_WIDTH + 2 * SSD_BC
REL_BUCKETS = 32
REL_MAX_DIST = 128

COL_S5 = 0
COL_DQ, COL_DK, COL_DV = 1, 2, 3
COL_MQ, COL_MK, COL_MV = 4, 5, 6
COL_Z = 7
COL_XBC = 4
PROJ_MAIN = 10 * GROUP_WIDTH

LANES = 128
SUBLANES = 8
ATT_BLOCK = 256
ATT_MAPS_PER_PASS = 8
GATE_ROWS = 16
LOG2E = math.log2(math.e)
S5_LANE_CHUNK = 512
S5_CHUNKS = (GROUP_WIDTH // S5_CH_PER_GROUP) * S5_STATE // S5_LANE_CHUNK
VMEM_LIMIT = 56 * 1024 * 1024


def _cparams(sem):
    return pltpu.CompilerParams(dimension_semantics=sem, vmem_limit_bytes=VMEM_LIMIT)


def _rmsnorm_kernel(x_ref, w_ref, o_ref):
    x = x_ref[...].astype(F32)
    ms = jnp.mean(x * x, axis=-1, keepdims=True)
    o_ref[...] = (x * lax.rsqrt(ms + EPS) * w_ref[...]).astype(o_ref.dtype)


def rmsnorm(x, w, out_dtype, tm=256):
    t, d = x.shape
    return pl.pallas_call(
        _rmsnorm_kernel,
        grid=(t // tm,),
        in_specs=[pl.BlockSpec((tm, d), lambda i: (i, 0)), pl.BlockSpec((1, d), lambda i: (0, 0))],
        out_specs=pl.BlockSpec((tm, d), lambda i: (i, 0)),
        out_shape=jax.ShapeDtypeStruct((t, d), out_dtype),
        compiler_params=_cparams(("parallel",)),
        name="rmsnorm",
    )(x, w.reshape(1, d).astype(F32))


def _mm_kernel(*refs, nk, act, n_lhs, has_res):
    a_refs = refs[:n_lhs]
    w_ref = refs[n_lhs]
    res_ref = refs[n_lhs + 1] if has_res else None
    o_ref = refs[n_lhs + 1 + int(has_res)]
    acc_ref = refs[n_lhs + 2 + int(has_res)]
    k = pl.program_id(2)

    def partial_dot():
        if n_lhs == 1:
            return jnp.dot(a_refs[0][...], w_ref[...], preferred_element_type=F32)
        kw = a_refs[0].shape[1]
        tot = None
        for i, a_ref in enumerate(a_refs):
            p = jnp.dot(a_ref[...], w_ref[i * kw:(i + 1) * kw, :], preferred_element_type=F32)
            tot = p if tot is None else tot + p
        return tot

    def finish(acc):
        if act == "relu2":
            r = jnp.maximum(acc, 0.0)
            acc = r * r
        if has_res:
            acc = res_ref[...] + acc
        o_ref[...] = acc.astype(o_ref.dtype)

    if nk == 1:
        finish(partial_dot())
        return

    @pl.when(k == 0)
    def _():
        acc_ref[...] = partial_dot()

    @pl.when(jnp.logical_and(k > 0, k < nk - 1))
    def _():
        acc_ref[...] += partial_dot()

    @pl.when(k == nk - 1)
    def _():
        finish(acc_ref[...] + partial_dot())


def matmul(lhs, w3, layer, n_out, out_dtype, act=None, residual=None, tm=1024, tn=1024, tk=2048):
    lhs = list(lhs)
    m = lhs[0].shape[0]
    kdim = w3.shape[1]
    tm, tn = min(tm, m), min(tn, n_out)
    if len(lhs) > 1:
        assert all(a.shape[1] * len(lhs) == kdim for a in lhs)
        tk = kdim
    tk = min(tk, kdim)
    nk = kdim // tk
    assert m % tm == 0 and n_out % tn == 0 and kdim % tk == 0
    in_specs = []
    for a in lhs:
        if len(lhs) == 1:
            in_specs.append(pl.BlockSpec((tm, tk), lambda i, j, k: (i, k)))
        else:
            in_specs.append(pl.BlockSpec((tm, a.shape[1]), lambda i, j, k: (i, 0)))
    in_specs.append(pl.BlockSpec((None, tk, tn), lambda i, j, k: (layer, k, j)))
    args = lhs + [w3]
    if residual is not None:
        in_specs.append(pl.BlockSpec((tm, tn), lambda i, j, k: (i, j)))
        args.append(residual)
    return pl.pallas_call(
        functools.partial(_mm_kernel, nk=nk, act=act, n_lhs=len(lhs), has_res=residual is not None),
        grid=(m // tm, n_out // tn, nk),
        in_specs=in_specs,
        out_specs=pl.BlockSpec((tm, tn), lambda i, j, k: (i, j)),
        out_shape=jax.ShapeDtypeStruct((m, n_out), out_dtype),
        scratch_shapes=[pltpu.VMEM((tm, tn) if nk > 1 else (8, LANES), F32)],
        compiler_params=_cparams(("parallel", "parallel", "arbitrary")),
        name="matmul_" + (act or "lin") + ("_res" if residual is not None else ""),
    )(*args)


def _mm_wstat_kernel(*refs, act, n_lhs, has_res):
    a_refs = refs[:n_lhs]
    w_ref = refs[n_lhs]
    res_ref = refs[n_lhs + 1] if has_res else None
    o_ref = refs[n_lhs + 1 + int(has_res)]
    wb_ref = refs[n_lhs + 2 + int(has_res)]

    @pl.when(pl.program_id(1) == 0)
    def _():
        wb_ref[...] = w_ref[...].astype(BF16)

    kw = a_refs[0].shape[1]
    acc = None
    for i, a_ref in enumerate(a_refs):
        p = jnp.dot(a_ref[...], wb_ref[i * kw:(i + 1) * kw, :], preferred_element_type=F32)
        acc = p if acc is None else acc + p
    if act == "relu2":
        r = jnp.maximum(acc, 0.0)
        acc = r * r
    if has_res:
        acc = res_ref[...] + acc
    o_ref[...] = acc.astype(o_ref.dtype)


def matmul_wstat(lhs, w3, layer, n_out, out_dtype, act=None, residual=None, tm=1024, tn=512):
    lhs = list(lhs)
    m = lhs[0].shape[0]
    kdim = w3.shape[1]
    assert sum(a.shape[1] for a in lhs) == kdim and all(a.shape[1] == lhs[0].shape[1] for a in lhs)
    assert m % tm == 0 and n_out % tn == 0
    in_specs = [pl.BlockSpec((tm, a.shape[1]), lambda j, i: (i, 0)) for a in lhs]
    in_specs.append(pl.BlockSpec((None, kdim, tn), lambda j, i: (layer, 0, j)))
    args = lhs + [w3]
    if residual is not None:
        in_specs.append(pl.BlockSpec((tm, tn), lambda j, i: (i, j)))
        args.append(residual)
    return pl.pallas_call(
        functools.partial(_mm_wstat_kernel, act=act, n_lhs=len(lhs), has_res=residual is not None),
        grid=(n_out // tn, m // tm),
        in_specs=in_specs,
        out_specs=pl.BlockSpec((tm, tn), lambda j, i: (i, j)),
        out_shape=jax.ShapeDtypeStruct((m, n_out), out_dtype),
        scratch_shapes=[pltpu.VMEM((kdim, tn), BF16)],
        compiler_params=_cparams(("arbitrary", "arbitrary")),
        name="matmul_wstat_" + (act or "lin") + ("_res" if residual is not None else ""),
    )(*args)


def _split3(x):
    hi = x.astype(BF16)
    r1 = x - hi.astype(F32)
    mid = r1.astype(BF16)
    lo = (r1 - mid.astype(F32)).astype(BF16)
    return hi, mid, lo


def _dot_exact_lhs(x, sel):
    hi, mid, lo = _split3(x)
    d = functools.partial(jnp.dot, preferred_element_type=F32)
    return d(hi, sel) + d(mid, sel) + d(lo, sel)


def _dot_exact_rhs(sel, x):
    hi, mid, lo = _split3(x)
    d = functools.partial(jnp.dot, preferred_element_type=F32)
    return d(sel, hi) + d(sel, mid) + d(sel, lo)


def _gelu_tanh(x):
    c = math.sqrt(2.0 / math.pi)
    return 0.5 * x * (1.0 + jnp.tanh(c * (x + 0.044715 * (x * x * x))))


def _s5_kernel(u_ref, perm_ref, permt_ref, bre_ref, bim_ref, cre_ref, cim_ref, are_ref, aim_ref,
               apr_ref, api_ref, d_ref, wglu_ref, nw_ref, o_ref, sr_ref, si_ref, y_ref, st_re, st_im, *, tc):
    nsub = SUBLANES
    m = tc // nsub

    @pl.when(pl.program_id(1) == 0)
    def _():
        st_re[...] = jnp.zeros_like(st_re)
        st_im[...] = jnp.zeros_like(st_im)

    up = jnp.dot(perm_ref[...], u_ref[...], preferred_element_type=F32).astype(BF16)

    cw = S5_LANE_CHUNK
    uw = cw // S5_STATE * S5_CH_PER_GROUP
    sub = lax.broadcasted_iota(jnp.int32, (nsub, cw), 0)
    for j in range(S5_CHUNKS):
        cols = slice(j * cw, (j + 1) * cw)
        uj = up[:, j * uw:(j + 1) * uw]
        sr_ref[...] = jnp.dot(uj, bre_ref[j], preferred_element_type=F32)
        si_ref[...] = jnp.dot(uj, bim_ref[j], preferred_element_type=F32)
        ar = jnp.broadcast_to(are_ref[:, cols], (nsub, cw))
        ai = jnp.broadcast_to(aim_ref[:, cols], (nsub, cw))

        def step(t, carry, ar=ar, ai=ai):
            pr, pi = carry
            rows = pl.ds(pl.multiple_of(t * nsub, nsub), nsub)
            nr = ar * pr - ai * pi + sr_ref[rows, :]
            ni = ar * pi + ai * pr + si_ref[rows, :]
            sr_ref[rows, :] = nr
            si_ref[rows, :] = ni
            return nr, ni

        zero = jnp.zeros((nsub, cw), F32)
        loc_r, loc_i = lax.fori_loop(0, m, step, (zero, zero), unroll=4)

        er, ei = st_re[:, cols], st_im[:, cols]
        amr, ami = apr_ref[m - 1:m, cols], api_ref[m - 1:m, cols]
        ent_r, ent_i = zero, zero
        for q in range(nsub):
            ent_r = jnp.where(sub == q, er, ent_r)
            ent_i = jnp.where(sub == q, ei, ent_i)
            er, ei = (amr * er - ami * ei + loc_r[q:q + 1, :], amr * ei + ami * er + loc_i[q:q + 1, :])
        st_re[:, cols] = er
        st_im[:, cols] = ei

        def fix(t, carry, ent_r=ent_r, ent_i=ent_i, cols=cols):
            rows = pl.ds(pl.multiple_of(t * nsub, nsub), nsub)
            pr = apr_ref[pl.ds(t, 1), cols]
            pi = api_ref[pl.ds(t, 1), cols]
            sr_ref[rows, :] = sr_ref[rows, :] + (pr * ent_r - pi * ent_i)
            si_ref[rows, :] = si_ref[rows, :] + (pr * ent_i + pi * ent_r)
            return carry

        lax.fori_loop(0, m, fix, 0, unroll=4)
        y_ref[:, j * uw:(j + 1) * uw] = (
            jnp.dot(sr_ref[...].astype(BF16), cre_ref[j], preferred_element_type=F32)
            - jnp.dot(si_ref[...].astype(BF16), cim_ref[j], preferred_element_type=F32))

    y = _gelu_tanh(y_ref[...] + d_ref[...] * up.astype(F32))
    gate = jnp.dot(y.astype(BF16), wglu_ref[...], preferred_element_type=F32)
    out = y * jax.nn.sigmoid(gate)
    ms = jnp.mean(out * out, axis=-1, keepdims=True)
    outp = (out * lax.rsqrt(ms + EPS) * nw_ref[...]).astype(BF16)
    o_ref[...] = jnp.dot(permt_ref[...], outp, preferred_element_type=F32).astype(o_ref.dtype)


def _s5_discretise(lam_re, lam_im, log_dt, b_re, b_im):
    dt = jnp.exp(log_dt.astype(F32))[:, None]
    lr, li = lam_re.astype(F32), lam_im.astype(F32)
    mag = jnp.exp(lr * dt)
    ab_re = mag * jnp.cos(li * dt)
    ab_im = mag * jnp.sin(li * dt)
    den = lr * lr + li * li
    f_re = ((ab_re - 1.0) * lr + ab_im * li) / den
    f_im = (ab_im * lr - (ab_re - 1.0) * li) / den
    br, bi = b_re.astype(F32), b_im.astype(F32)
    bb_re = f_re[..., None] * br - f_im[..., None] * bi
    bb_im = f_re[..., None] * bi + f_im[..., None] * br
    return ab_re, ab_im, bb_re, bb_im


def _block_diag(blocks, per):
    g, r, c = blocks.shape
    b = blocks.reshape(g // per, per, r, c)
    eye = jnp.eye(per, dtype=blocks.dtype)
    return jnp.einsum("nirc,ij->nirjc", b, eye).reshape(g // per, per * r, per * c)


def s5_mixer(proj, bsz, seq, lam_re, lam_im, log_dt, b_re, b_im, c_re, c_im, d_skip, w_glu, norm_w, tc=256):
    per = S5_LANE_CHUNK // S5_STATE
    ab_re, ab_im, bb_re, bb_im = _s5_discretise(lam_re, lam_im, log_dt, b_re, b_im)
    bre = _block_diag(jnp.swapaxes(bb_re, 1, 2), per).astype(BF16)
    bim = _block_diag(jnp.swapaxes(bb_im, 1, 2), per).astype(BF16)
    cre = _block_diag(jnp.swapaxes(c_re.astype(F32), 1, 2), per).astype(BF16)
    cim = _block_diag(jnp.swapaxes(c_im.astype(F32), 1, 2), per).astype(BF16)
    nstate = ab_re.size
    tc = min(tc, seq)
    nct = seq // tc
    m = tc // SUBLANES
    steps = jnp.arange(1, m + 1, dtype=F32)[:, None, None]
    dt = jnp.exp(log_dt.astype(F32))[None, :, None]
    pow_mag = jnp.exp(steps * lam_re.astype(F32)[None] * dt)
    pow_ang = steps * lam_im.astype(F32)[None] * dt
    apow_re = (pow_mag * jnp.cos(pow_ang)).reshape(m, nstate)
    apow_im = (pow_mag * jnp.sin(pow_ang)).reshape(m, nstate)
    r = jnp.arange(tc)
    perm = (r[None, :] == ((r % SUBLANES) * m + r // SUBLANES)[:, None]).astype(BF16)
    uw = per * S5_CH_PER_GROUP
    const2 = lambda b, c: (0, 0)
    const3 = lambda b, c: (0, 0, 0)
    return pl.pallas_call(
        functools.partial(_s5_kernel, tc=tc),
        grid=(bsz, nct),
        in_specs=[
            pl.BlockSpec((tc, GROUP_WIDTH), lambda b, c: (b * nct + c, COL_S5)),
            pl.BlockSpec((tc, tc), const2),
            pl.BlockSpec((tc, tc), const2),
            pl.BlockSpec((S5_CHUNKS, uw, S5_LANE_CHUNK), const3),
            pl.BlockSpec((S5_CHUNKS, uw, S5_LANE_CHUNK), const3),
            pl.BlockSpec((S5_CHUNKS, S5_LANE_CHUNK, uw), const3),
            pl.BlockSpec((S5_CHUNKS, S5_LANE_CHUNK, uw), const3),
            pl.BlockSpec((1, nstate), const2),
            pl.BlockSpec((1, nstate), const2),
            pl.BlockSpec((m, nstate), const2),
            pl.BlockSpec((m, nstate), const2),
            pl.BlockSpec((1, GROUP_WIDTH), const2),
            pl.BlockSpec((GROUP_WIDTH, GROUP_WIDTH), const2),
            pl.BlockSpec((1, GROUP_WIDTH), const2),
        ],
        out_specs=pl.BlockSpec((tc, GROUP_WIDTH), lambda b, c: (b * nct + c, 0)),
        out_shape=jax.ShapeDtypeStruct((bsz * seq, GROUP_WIDTH), BF16),
        scratch_shapes=[
            pltpu.VMEM((tc, S5_LANE_CHUNK), F32),
            pltpu.VMEM((tc, S5_LANE_CHUNK), F32),
            pltpu.VMEM((tc, GROUP_WIDTH), F32),
            pltpu.VMEM((1, nstate), F32),
            pltpu.VMEM((1, nstate), F32),
        ],
        compiler_params=_cparams(("parallel", "arbitrary")),
        name="s5_mixer",
    )(proj, perm, perm.T, bre, bim, cre, cim, ab_re.reshape(1, nstate), ab_im.reshape(1, nstate),
      apow_re, apow_im, d_skip.reshape(1, GROUP_WIDTH).astype(F32), w_glu.astype(BF16), norm_w.reshape(1, GROUP_WIDTH).astype(F32))


def _rel_bucket(dist):
    n = jnp.maximum(dist, 0)
    max_exact = REL_BUCKETS // 2
    log_ratio = jnp.log(jnp.maximum(n, 1).astype(F32) / max_exact) / math.log(REL_MAX_DIST / max_exact)
    large = max_exact + (log_ratio * (REL_BUCKETS - max_exact)).astype(jnp.int32)
    large = jnp.minimum(large, REL_BUCKETS - 1)
    return jnp.where(n < max_exact, n, large)


def _bias_tiles(tbl, blk):
    assert blk >= REL_MAX_DIST
    i = jnp.arange(blk)[:, None]
    j = jnp.arange(blk)[None, :]
    buckets = jnp.stack([_rel_bucket(i - j), _rel_bucket(blk + i - j), _rel_bucket(jnp.full((blk, blk), 2 * blk))])
    onehot = (buckets[..., None] == jnp.arange(REL_BUCKETS)).astype(F32)
    tiles = jnp.einsum("tijk,hk->htij", onehot, tbl, precision=lax.Precision.HIGHEST) * LOG2E
    visible = jnp.stack([i >= j, jnp.ones((blk, blk), bool), jnp.ones((blk, blk), bool)])
    return jnp.where(visible[None], tiles, NEG_INF).astype(F32)


def _qk(q, k):
    return lax.dot_general(q, k, (((1,), (1,)), ((), ())), preferred_element_type=F32)


def _attn_kernel(q_ref, k_ref, v_ref, bias_ref, p0_ref, p1_ref, o_ref,
                 qs_ref, s_ref, mx_ref, mb_ref, ls_ref, acc_ref, kmean_ref, out_ref, *, mode, scale, heads, group):
    blk = ATT_BLOCK
    own = pl.program_id(1)
    lane = lax.broadcasted_iota(jnp.int32, (blk, LANES), 1)
    n_maps = 2 if mode == "diff" else 1
    nb = k_ref.shape[0] // blk
    reps = blk // LANES

    if mode == "moba":
        @pl.when(own == 0)
        def _():
            for h in range(heads):
                kf = k_ref[:, h * LANES:(h + 1) * LANES].astype(F32).reshape(nb, blk, LANES)
                kmean = jnp.sum(kf, axis=1) * (1.0 / blk)
                kmean_ref[h] = jnp.concatenate([kmean, jnp.zeros((GATE_ROWS - nb, LANES), F32)], axis=0)

    for g0 in range(0, heads, group):
        maps = [(h, mi) for h in range(g0, g0 + group) for mi in range(n_maps)]
        for h in range(g0, g0 + group):
            hs = slice(h * LANES, (h + 1) * LANES)
            q = q_ref[:, hs]
            if mode == "diff":
                qf = q.astype(F32)
                qs_ref[(h - g0) * 2] = jnp.where(lane < DIFF_HEAD_DIM, qf, 0.0).astype(BF16)
                qs_ref[(h - g0) * 2 + 1] = jnp.where(lane >= DIFF_HEAD_DIM, qf, 0.0).astype(BF16)
            else:
                hi, mid, lo = _split3(kmean_ref[h])
                gate = _qk(hi, q) + _qk(mid, q) + _qk(lo, q)
                sub = lax.broadcasted_iota(jnp.int32, (GATE_ROWS, blk), 0)
                gate = jnp.where(sub < own, gate, NEG_INF)
                penalty = jnp.zeros((GATE_ROWS, blk), F32)
                for n in range(nb):
                    row = gate[n:n + 1, :]
                    beats = jnp.logical_and(
                        jnp.logical_or(gate > row, jnp.logical_and(gate == row, sub < n)), sub < nb)
                    rank = jnp.sum(beats.astype(F32), axis=0, keepdims=True)
                    keep = jnp.logical_or(jnp.logical_and(rank < MOBA_TOPK, n < own), n == own)
                    penalty = jnp.where(sub == n, jnp.where(keep, 0.0, NEG_INF), penalty)
                pen_t = jnp.transpose(
                    jnp.concatenate([penalty, jnp.zeros((LANES - GATE_ROWS, blk), F32)], axis=0))
                qs_ref[h - g0] = jnp.concatenate([q, pen_t.astype(BF16)], axis=1)

        for idx in range(len(maps)):
            mx_ref[idx] = jnp.full((blk, LANES), -jnp.inf, F32)

        def logits_pass(n, carry, g0=g0, maps=maps):
            start = pl.multiple_of(n * blk, blk)
            tile_dist = jnp.minimum(own - n, 2)
            if mode == "moba":
                onehot = jnp.where(lane == n, 1.0, 0.0).astype(BF16)
            for idx, (h, mi) in enumerate(maps):
                hs = slice(h * LANES, (h + 1) * LANES)
                kn = k_ref[pl.ds(start, blk), hs]
                if mode == "moba":
                    kn = jnp.concatenate([kn, onehot], axis=1)
                s = _qk(qs_ref[idx], kn) * (scale * LOG2E) + bias_ref[h, tile_dist]
                s_ref[idx, n] = s
                part = s[:, :LANES]
                for r in range(1, reps):
                    part = jnp.maximum(part, s[:, r * LANES:(r + 1) * LANES])
                mx_ref[idx] = jnp.maximum(mx_ref[idx], part)
            return carry

        lax.fori_loop(0, own + 1, logits_pass, 0)

        for idx in range(len(maps)):
            mb_ref[idx] = jnp.broadcast_to(jnp.max(mx_ref[idx], axis=-1, keepdims=True), (blk, LANES))
            ls_ref[idx] = jnp.zeros((blk, LANES), F32)
            acc_ref[idx] = jnp.zeros((blk, LANES), F32)

        def value_pass(n, carry, maps=maps):
            start = pl.multiple_of(n * blk, blk)
            for idx, (h, mi) in enumerate(maps):
                hs = slice(h * LANES, (h + 1) * LANES)
                s = s_ref[idx, n]
                mb = mb_ref[idx]
                ps = [jnp.exp2(s[:, r * LANES:(r + 1) * LANES] - mb) for r in range(reps)]
                tot = ps[0]
                for r in range(1, reps):
                    tot = tot + ps[r]
                ls_ref[idx] += tot
                p = jnp.concatenate(ps, axis=1).astype(BF16)
                acc_ref[idx] += jnp.dot(p, v_ref[pl.ds(start, blk), hs], preferred_element_type=F32)
            return carry

        lax.fori_loop(0, own + 1, value_pass, 0)

        for h in range(g0, g0 + group):
            hs = slice(h * LANES, (h + 1) * LANES)
            i0 = (h - g0) * n_maps
            o = acc_ref[i0] / jnp.sum(ls_ref[i0], axis=-1, keepdims=True)
            if mode == "diff":
                o2 = acc_ref[i0 + 1] / jnp.sum(ls_ref[i0 + 1], axis=-1, keepdims=True)
                o = o - p0_ref[...] * o2
                ms = jnp.mean(o * o, axis=-1, keepdims=True)
                o_ref[:, hs] = (o * lax.rsqrt(ms + EPS) * p1_ref[...]).astype(o_ref.dtype)
            else:
                out_ref[:, hs] = o

    if mode == "moba":
        o = out_ref[...]
        ms = jnp.mean(o * o, axis=-1, keepdims=True)
        o_ref[...] = (o * lax.rsqrt(ms + EPS) * p1_ref[...]).astype(o_ref.dtype)


def attention(proj, bsz, seq, col_q, col_k, col_v, bias, p0, p1, mode, scale, heads):
    blk = ATT_BLOCK
    nq = seq // blk
    width = heads * LANES
    assert width == GROUP_WIDTH and seq % blk == 0
    nb = seq // blk
    assert nb <= GATE_ROWS
    n_maps = 2 if mode == "diff" else 1
    group = ATT_MAPS_PER_PASS // n_maps
    const2 = lambda b, i: (0, 0)
    return pl.pallas_call(
        functools.partial(_attn_kernel, mode=mode, scale=scale, heads=heads, group=group),
        grid=(bsz, nq),
        in_specs=[
            pl.BlockSpec((blk, width), lambda b, i: (b * nq + i, col_q)),
            pl.BlockSpec((seq, width), lambda b, i: (b, col_k)),
            pl.BlockSpec((seq, width), lambda b, i: (b, col_v)),
            pl.BlockSpec((heads, 3, blk, blk), lambda b, i: (0, 0, 0, 0)),
            pl.BlockSpec(p0.shape, const2),
            pl.BlockSpec(p1.shape, const2),
        ],
        out_specs=pl.BlockSpec((blk, width), lambda b, i: (b * nq + i, 0)),
        out_shape=jax.ShapeDtypeStruct((bsz * seq, width), BF16),
        scratch_shapes=[
            pltpu.VMEM((ATT_MAPS_PER_PASS, blk, LANES * (1 if mode == "diff" else 2)), BF16),
            pltpu.VMEM((ATT_MAPS_PER_PASS, nb, blk, blk), F32),
            pltpu.VMEM((ATT_MAPS_PER_PASS, blk, LANES), F32),
            pltpu.VMEM((ATT_MAPS_PER_PASS, blk, LANES), F32),
            pltpu.VMEM((ATT_MAPS_PER_PASS, blk, LANES), F32),
            pltpu.VMEM((ATT_MAPS_PER_PASS, blk, LANES), F32),
            pltpu.VMEM((heads, GATE_ROWS, LANES), F32),
            pltpu.VMEM((blk, width), F32),
        ],
        compiler_params=_cparams(("parallel", "arbitrary")),
        name="attn_" + mode,
    )(proj, proj, proj, bias, p0, p1)


def _softplus(x):
    return jnp.maximum(x, 0.0) + jnp.log(1.0 + jnp.exp(-jnp.abs(x)))


def _silu(x):
    return x * jax.nn.sigmoid(x)


def _ssd_kernel(z_ref, xbc_ref, dt_ref, dtt_ref, cw_ref, cb_ref, dtb_ref, dtbt_ref, a_ref, at_ref, dsk_ref,
                nw_ref, o_ref, ext_ref, st_ref, y_ref, *, lc):
    halo = 8
    c = pl.program_id(1)

    @pl.when(c == 0)
    def _():
        ext_ref[0:halo, :] = jnp.zeros((halo, SSD_XBC), F32)
        st_ref[...] = jnp.zeros_like(st_ref)

    ext_ref[halo:halo + lc, :] = xbc_ref[...].astype(F32)
    conv = cb_ref[...] + cw_ref[SSD_CONV - 1:SSD_CONV, :] * ext_ref[halo:halo + lc, :]
    for kk in range(1, SSD_CONV):
        conv = conv + cw_ref[SSD_CONV - 1 - kk:SSD_CONV - kk, :] * ext_ref[halo - kk:halo - kk + lc, :]
    ext_ref[0:halo, :] = ext_ref[lc:lc + halo, :]
    xbc = _silu(conv)
    xs = xbc[:, :GROUP_WIDTH]

    dt_c = _softplus(dt_ref[...] + dtb_ref[...])
    dt_r = _softplus(dtt_ref[...] + dtbt_ref[...])
    a_c = dt_c * a_ref[...]
    a_r = dt_r * at_ref[...]
    row = lax.broadcasted_iota(jnp.int32, (lc, lc), 0)
    colm = lax.broadcasted_iota(jnp.int32, (lc, lc), 1)
    causal = row >= colm
    tri = causal.astype(BF16)
    tri_t = (colm >= row).astype(BF16)
    acum_c = _dot_exact_rhs(tri, a_c)
    acum_r = _dot_exact_lhs(a_r, tri_t)
    alast_c = acum_c[lc - 1:lc, :]

    lane = lax.broadcasted_iota(jnp.int32, (lc, LANES), 1)
    lo_half = lane < SSD_HEAD_DIM
    heads_per_group = SSD_HEADS // SSD_GROUPS
    for g in range(SSD_GROUPS):
        bg = xbc[:, GROUP_WIDTH + g * SSD_STATE:GROUP_WIDTH + (g + 1) * SSD_STATE].astype(BF16)
        cg = xbc[:, GROUP_WIDTH + SSD_BC + g * SSD_STATE:GROUP_WIDTH + SSD_BC + (g + 1) * SSD_STATE].astype(BF16)
        cb = _qk(cg, bg)
        for pr in range(heads_per_group // 2):
            ha = g * heads_per_group + 2 * pr
            hb = ha + 1
            tile = slice(ha * SSD_HEAD_DIM, (ha + 2) * SSD_HEAD_DIM)
            xp = xs[:, tile]

            def per_head(col_a, col_b):
                return jnp.where(lo_half, col_a, col_b)

            xdt = xp * per_head(dt_c[:, ha:ha + 1], dt_c[:, hb:hb + 1])
            ydiag = None
            for hh, keep in ((ha, lo_half), (hb, jnp.logical_not(lo_half))):
                seg = acum_c[:, hh:hh + 1] - acum_r[hh:hh + 1, :]
                decay = jnp.where(causal, jnp.exp(jnp.where(causal, seg, 0.0)), 0.0)
                mm = (cb * decay).astype(BF16)
                part = jnp.dot(mm, jnp.where(keep, xdt, 0.0).astype(BF16), preferred_element_type=F32)
                ydiag = part if ydiag is None else ydiag + part
            st = st_ref[ha // 2]
            yoff = _qk(cg, st.astype(BF16)) * per_head(jnp.exp(acum_c[:, ha:ha + 1]), jnp.exp(acum_c[:, hb:hb + 1]))
            y_ref[:, tile] = ydiag + yoff + dsk_ref[:, tile] * xp
            to_end = per_head(jnp.exp(alast_c[:, ha:ha + 1] - acum_c[:, ha:ha + 1]),
                              jnp.exp(alast_c[:, hb:hb + 1] - acum_c[:, hb:hb + 1]))
            xdec_t = jnp.transpose(xdt * to_end).astype(BF16)
            sub = lax.broadcasted_iota(jnp.int32, (LANES, SSD_STATE), 0)
            chunk_decay = jnp.where(sub < SSD_HEAD_DIM, jnp.exp(alast_c[:, ha:ha + 1]),
                                    jnp.exp(alast_c[:, hb:hb + 1]))
            st_ref[ha // 2] = st * chunk_decay + jnp.dot(xdec_t, bg, preferred_element_type=F32)

    y = y_ref[...] * _silu(z_ref[...].astype(F32))
    gw = GROUP_WIDTH // SSD_GROUPS
    for g in range(SSD_GROUPS):
        yg = y[:, g * gw:(g + 1) * gw]
        ms = jnp.mean(yg * yg, axis=-1, keepdims=True)
        o_ref[:, g * gw:(g + 1) * gw] = (yg * lax.rsqrt(ms + EPS) * nw_ref[:, g * gw:(g + 1) * gw]).astype(o_ref.dtype)


def _pad_lanes(v):
    return jnp.pad(v.astype(F32), (0, LANES - v.shape[0])).reshape(1, LANES)


def ssd_mixer(proj, dt_raw, bsz, seq, conv_w, conv_b, dt_bias, a_log, d_skip, norm_w, lc=128):
    nc = seq // lc
    t = bsz * seq
    a = -jnp.exp(a_log.astype(F32))
    dt_t = jnp.transpose(dt_raw[:, :SSD_HEADS])
    col16 = lambda v: jnp.broadcast_to(v.astype(F32)[:, None], (SSD_HEADS, LANES))
    dskip = jnp.repeat(d_skip.astype(F32), SSD_HEAD_DIM).reshape(1, GROUP_WIDTH)
    const2 = lambda b, c: (0, 0)
    return pl.pallas_call(
        functools.partial(_ssd_kernel, lc=lc),
        grid=(bsz, nc),
        in_specs=[
            pl.BlockSpec((lc, GROUP_WIDTH), lambda b, c: (b * nc + c, COL_Z)),
            pl.BlockSpec((lc, SSD_XBC), lambda b, c: (b * nc + c, COL_XBC)),
            pl.BlockSpec((lc, LANES), lambda b, c: (b * nc + c, 0)),
            pl.BlockSpec((SSD_HEADS, lc), lambda b, c: (0, b * nc + c)),
            pl.BlockSpec((SSD_CONV, SSD_XBC), const2),
            pl.BlockSpec((1, SSD_XBC), const2),
            pl.BlockSpec((1, LANES), const2),
            pl.BlockSpec((SSD_HEADS, LANES), const2),
            pl.BlockSpec((1, LANES), const2),
            pl.BlockSpec((SSD_HEADS, LANES), const2),
            pl.BlockSpec((1, GROUP_WIDTH), const2),
            pl.BlockSpec((1, GROUP_WIDTH), const2),
        ],
        out_specs=pl.BlockSpec((lc, GROUP_WIDTH), lambda b, c: (b * nc + c, 0)),
        out_shape=jax.ShapeDtypeStruct((t, GROUP_WIDTH), BF16),
        scratch_shapes=[
            pltpu.VMEM((lc + 8, SSD_XBC), F32),
            pltpu.VMEM((SSD_HEADS // 2, 2 * SSD_HEAD_DIM, SSD_STATE), F32),
            pltpu.VMEM((lc, GROUP_WIDTH), F32),
        ],
        compiler_params=_cparams(("parallel", "arbitrary")),
        name="ssd_mixer",
    )(proj, proj, dt_raw, dt_t, conv_w.reshape(SSD_CONV, SSD_XBC).astype(F32),
      conv_b.reshape(1, SSD_XBC).astype(F32), _pad_lanes(dt_bias), col16(dt_bias), _pad_lanes(a), col16(a),
      dskip, norm_w.reshape(1, GROUP_WIDTH).astype(F32))


def kernel(x, rel_bias_table, attn_norm_w, w_in, s5_lam_re, s5_lam_im, s5_log_dt, s5_b_re, s5_b_im, s5_c_re, s5_c_im, s5_d, s5_w_glu, s5_out_norm_w, diff_lam_q1, diff_lam_k1, diff_lam_q2, diff_lam_k2, diff_subln_w, moba_out_norm_w, ssd_conv_w, ssd_conv_b, ssd_dt_bias, ssd_a_log, ssd_d, ssd_norm_w, w_out, mlp_norm_w, w_up, w_down, final_norm_w):
    bsz, seq, d_model = x.shape
    depth = w_in.shape[0]
    t = bsz * seq
    x = x.reshape(t, d_model).astype(F32)

    w_down_b = w_down.astype(BF16)
    w_dt_b = jnp.pad(w_in[:, :, PROJ_MAIN:], ((0, 0), (0, 0), (0, LANES - SSD_HEADS))).astype(BF16)

    tbl = rel_bias_table.astype(F32).T
    bias_diff = _bias_tiles(tbl[:DIFF_HEADS], ATT_BLOCK)
    bias_moba = _bias_tiles(tbl[DIFF_HEADS:], ATT_BLOCK)

    for l in range(depth):
        h = rmsnorm(x, attn_norm_w[l], BF16)
        proj = matmul_wstat([h], w_in, l, PROJ_MAIN, BF16)
        dt_raw = matmul([h], w_dt_b, l, LANES, F32)

        y_s5 = s5_mixer(proj, bsz, seq, s5_lam_re[l], s5_lam_im[l], s5_log_dt[l], s5_b_re[l], s5_b_im[l],
                        s5_c_re[l], s5_c_im[l], s5_d[l], s5_w_glu[l], s5_out_norm_w[l])

        lam_init = 0.8 - 0.6 * math.exp(-0.3 * l)
        lam = (jnp.exp(jnp.sum(diff_lam_q1[l].astype(F32) * diff_lam_k1[l].astype(F32)))
               - jnp.exp(jnp.sum(diff_lam_q2[l].astype(F32) * diff_lam_k2[l].astype(F32))) + lam_init)
        y_diff = attention(proj, bsz, seq, COL_DQ, COL_DK, COL_DV, bias_diff,
                           jnp.broadcast_to(lam, (1, LANES)).astype(F32),
                           (diff_subln_w[l].astype(F32) * (1.0 - lam_init)).reshape(1, LANES),
                           "diff", DIFF_HEAD_DIM ** -0.5, DIFF_HEADS)
        y_moba = attention(proj, bsz, seq, COL_MQ, COL_MK, COL_MV, bias_moba,
                           jnp.zeros((1, LANES), F32),
                           moba_out_norm_w[l].astype(F32).reshape(1, GROUP_WIDTH),
                           "moba", (GROUP_WIDTH // MOBA_HEADS) ** -0.5, MOBA_HEADS)
        y_ssd = ssd_mixer(proj, dt_raw, bsz, seq, ssd_conv_w[l], ssd_conv_b[l], ssd_dt_bias[l], ssd_a_log[l],
                          ssd_d[l], ssd_norm_w[l])

        x = matmul_wstat([y_s5, y_diff, y_moba, y_ssd], w_out, l, d_model, F32, residual=x)
        h = rmsnorm(x, mlp_norm_w[l], BF16)
        u = matmul_wstat([h], w_up, l, w_up.shape[2], BF16, act="relu2")
        x = matmul([u], w_down_b, l, d_model, F32, residual=x)

    return rmsnorm(x, final_norm_w, F32).reshape(bsz, seq, d_model)
```

```python
import functools
import math

import jax
import jax.numpy as jnp
from jax import lax
from jax.experimental import pallas as pl
from jax.experimental.pallas import tpu as pltpu

F32 = jnp.float32
BF16 = jnp.bfloat16
EPS = 1e-6
NEG_INF = -1e30

GROUP_WIDTH = 1024
S5_CH_PER_GROUP = 16
S5_STATE = 64
DIFF_HEADS = 8
DIFF_HEAD_DIM = 64
MOBA_HEADS = 8
MOBA_BLOCK = 256
MOBA_TOPK = 3
SSD_HEAD_DIM = 64
SSD_HEADS = 16
SSD_GROUPS = 4
SSD_STATE = 128
SSD_CONV = 4
SSD_BC = SSD_GROUPS * SSD_STATE
SSD_XBC = GROUP_WIDTH + 2 * SSD_BC
REL_BUCKETS = 32
REL_MAX_DIST = 128

COL_S5 = 0
COL_DQ, COL_DK, COL_DV = 1, 2, 3
COL_MQ, COL_MK, COL_MV = 4, 5, 6
COL_Z = 7
COL_XBC = 4
PROJ_MAIN = 10 * GROUP_WIDTH

LANES = 128
SUBLANES = 8
ATT_BLOCK = 256
ATT_MAPS_PER_PASS = 8
GATE_ROWS = 16
LOG2E = math.log2(math.e)
S5_LANE_CHUNK = 512
S5_CHUNKS = (GROUP_WIDTH // S5_CH_PER_GROUP) * S5_STATE // S5_LANE_CHUNK
VMEM_LIMIT = 56 * 1024 * 1024


def _cparams(sem):
    return pltpu.CompilerParams(dimension_semantics=sem, vmem_limit_bytes=VMEM_LIMIT)


def _rmsnorm_kernel(x_ref, w_ref, o_ref):
    x = x_ref[...].astype(F32)
    ms = jnp.mean(x * x, axis=-1, keepdims=True)
    o_ref[...] = (x * lax.rsqrt(ms + EPS) * w_ref[...]).astype(o_ref.dtype)


def rmsnorm(x, w, out_dtype, tm=256):
    t, d = x.shape
    return pl.pallas_call(
        _rmsnorm_kernel,
        grid=(t // tm,),
        in_specs=[pl.BlockSpec((tm, d), lambda i: (i, 0)), pl.BlockSpec((1, d), lambda i: (0, 0))],
        out_specs=pl.BlockSpec((tm, d), lambda i: (i, 0)),
        out_shape=jax.ShapeDtypeStruct((t, d), out_dtype),
        compiler_params=_cparams(("parallel",)),
        name="rmsnorm",
    )(x, w.reshape(1, d).astype(F32))


def _mm_kernel(*refs, nk, act, n_lhs, has_res):
    a_refs = refs[:n_lhs]
    w_ref = refs[n_lhs]
    res_ref = refs[n_lhs + 1] if has_res else None
    o_ref = refs[n_lhs + 1 + int(has_res)]
    acc_ref = refs[n_lhs + 2 + int(has_res)]
    k = pl.program_id(2)

    def partial_dot():
        if n_lhs == 1:
            return jnp.dot(a_refs[0][...], w_ref[...], preferred_element_type=F32)
        kw = a_refs[0].shape[1]
        tot = None
        for i, a_ref in enumerate(a_refs):
            p = jnp.dot(a_ref[...], w_ref[i * kw:(i + 1) * kw, :], preferred_element_type=F32)
            tot = p if tot is None else tot + p
        return tot

    def finish(acc):
        if act == "relu2":
            r = jnp.maximum(acc, 0.0)
            acc = r * r
        if has_res:
            acc = res_ref[...] + acc
        o_ref[...] = acc.astype(o_ref.dtype)

    if nk == 1:
        finish(partial_dot())
        return

    @pl.when(k == 0)
    def _():
        acc_ref[...] = partial_dot()

    @pl.when(jnp.logical_and(k > 0, k < nk - 1))
    def _():
        acc_ref[...] += partial_dot()

    @pl.when(k == nk - 1)
    def _():
        finish(acc_ref[...] + partial_dot())


def matmul(lhs, w3, layer, n_out, out_dtype, act=None, residual=None, tm=1024, tn=1024, tk=2048):
    lhs = list(lhs)
    m = lhs[0].shape[0]
    kdim = w3.shape[1]
    tm, tn = min(tm, m), min(tn, n_out)
    if len(lhs) > 1:
        assert all(a.shape[1] * len(lhs) == kdim for a in lhs)
        tk = kdim
    tk = min(tk, kdim)
    nk = kdim // tk
    assert m % tm == 0 and n_out % tn == 0 and kdim % tk == 0
    in_specs = []
    for a in lhs:
        if len(lhs) == 1:
            in_specs.append(pl.BlockSpec((tm, tk), lambda i, j, k: (i, k)))
        else:
            in_specs.append(pl.BlockSpec((tm, a.shape[1]), lambda i, j, k: (i, 0)))
    in_specs.append(pl.BlockSpec((None, tk, tn), lambda i, j, k: (layer, k, j)))
    args = lhs + [w3]
    if residual is not None:
        in_specs.append(pl.BlockSpec((tm, tn), lambda i, j, k: (i, j)))
        args.append(residual)
    return pl.pallas_call(
        functools.partial(_mm_kernel, nk=nk, act=act, n_lhs=len(lhs), has_res=residual is not None),
        grid=(m // tm, n_out // tn, nk),
        in_specs=in_specs,
        out_specs=pl.BlockSpec((tm, tn), lambda i, j, k: (i, j)),
        out_shape=jax.ShapeDtypeStruct((m, n_out), out_dtype),
        scratch_shapes=[pltpu.VMEM((tm, tn) if nk > 1 else (8, LANES), F32)],
        compiler_params=_cparams(("parallel", "parallel", "arbitrary")),
        name="matmul_" + (act or "lin") + ("_res" if residual is not None else ""),
    )(*args)


def _mm_wstat_kernel(*refs, act, n_lhs, has_res, has_cast, w_transposed):
    a_refs = refs[:n_lhs]
    w_ref = refs[n_lhs]
    pos = n_lhs + 1
    res_ref = refs[pos] if has_res else None
    pos += int(has_res)
    cast_in_ref = refs[pos] if has_cast else None
    pos += int(has_cast)
    o_ref = refs[pos]
    cast_out_ref = refs[pos + 1] if has_cast else None
    wb_ref = refs[pos + 1 + int(has_cast)]

    @pl.when(pl.program_id(1) == 0)
    def _():
        wb_ref[...] = w_ref[...].astype(BF16)

    if has_cast:
        cast_out_ref[...] = cast_in_ref[...].astype(BF16)

    kw = a_refs[0].shape[1]
    acc = None
    for i, a_ref in enumerate(a_refs):
        if w_transposed:
            p = lax.dot_general(a_ref[...], wb_ref[:, i * kw:(i + 1) * kw], (((1,), (1,)), ((), ())),
                                preferred_element_type=F32)
        else:
            p = jnp.dot(a_ref[...], wb_ref[i * kw:(i + 1) * kw, :], preferred_element_type=F32)
        acc = p if acc is None else acc + p
    if act == "relu2":
        r = jnp.maximum(acc, 0.0)
        acc = r * r
    if has_res:
        acc = res_ref[...] + acc
    o_ref[...] = acc.astype(o_ref.dtype)


def matmul_wstat(lhs, w3, layer, n_out, out_dtype, act=None, residual=None, w_transposed=False, cast_through=None,
                 tm=1024, tn=512):
    lhs = list(lhs)
    m = lhs[0].shape[0]
    kdim = w3.shape[2 if w_transposed else 1]
    assert sum(a.shape[1] for a in lhs) == kdim and all(a.shape[1] == lhs[0].shape[1] for a in lhs)
    assert m % tm == 0 and n_out % tn == 0
    ni = m // tm
    in_specs = [pl.BlockSpec((tm, a.shape[1]), lambda j, i: (i, 0)) for a in lhs]
    if w_transposed:
        in_specs.append(pl.BlockSpec((None, tn, kdim), lambda j, i: (layer, j, 0)))
    else:
        in_specs.append(pl.BlockSpec((None, kdim, tn), lambda j, i: (layer, 0, j)))
    args = lhs + [w3]
    if residual is not None:
        in_specs.append(pl.BlockSpec((tm, tn), lambda j, i: (i, j)))
        args.append(residual)
    out_specs = pl.BlockSpec((tm, tn), lambda j, i: (i, j))
    out_shape = jax.ShapeDtypeStruct((m, n_out), out_dtype)
    if cast_through is not None:
        steps = (n_out // tn) * ni
        rows, cols = cast_through.shape[1:]
        assert rows % steps == 0
        slab = rows // steps
        in_specs.append(pl.BlockSpec((None, slab, cols), lambda j, i: (layer, j * ni + i, 0)))
        args.append(cast_through)
        out_specs = [out_specs, pl.BlockSpec((slab, cols), lambda j, i: (j * ni + i, 0))]
        out_shape = [out_shape, jax.ShapeDtypeStruct((rows, cols), BF16)]
    return pl.pallas_call(
        functools.partial(_mm_wstat_kernel, act=act, n_lhs=len(lhs), has_res=residual is not None,
                          has_cast=cast_through is not None, w_transposed=w_transposed),
        grid=(n_out // tn, ni),
        in_specs=in_specs,
        out_specs=out_specs,
        out_shape=out_shape,
        scratch_shapes=[pltpu.VMEM((tn, kdim) if w_transposed else (kdim, tn), BF16)],
        compiler_params=_cparams(("arbitrary", "arbitrary")),
        name="matmul_wstat_" + (act or "lin") + ("_res" if residual is not None else ""),
    )(*args)


def _split3(x):
    hi = x.astype(BF16)
    r1 = x - hi.astype(F32)
    mid = r1.astype(BF16)
    lo = (r1 - mid.astype(F32)).astype(BF16)
    return hi, mid, lo


def _dot_exact_lhs(x, sel):
    hi, mid, lo = _split3(x)
    d = functools.partial(jnp.dot, preferred_element_type=F32)
    return d(hi, sel) + d(mid, sel) + d(lo, sel)


def _dot_exact_rhs(sel, x):
    hi, mid, lo = _split3(x)
    d = functools.partial(jnp.dot, preferred_element_type=F32)
    return d(sel, hi) + d(sel, mid) + d(sel, lo)


def _gelu_tanh(x):
    c = math.sqrt(2.0 / math.pi)
    return 0.5 * x * (1.0 + jnp.tanh(c * (x + 0.044715 * (x * x * x))))


def _s5_kernel(u_ref, perm_ref, permt_ref, bre_ref, bim_ref, cre_ref, cim_ref, are_ref, aim_ref,
               apr_ref, api_ref, d_ref, wglu_ref, nw_ref, o_ref, sr_ref, si_ref, y_ref, st_re, st_im, *, tc):
    nsub = SUBLANES
    m = tc // nsub

    @pl.when(pl.program_id(1) == 0)
    def _():
        st_re[...] = jnp.zeros_like(st_re)
        st_im[...] = jnp.zeros_like(st_im)

    up = jnp.dot(perm_ref[...], u_ref[...], preferred_element_type=F32).astype(BF16)

    cw = S5_LANE_CHUNK
    uw = cw // S5_STATE * S5_CH_PER_GROUP
    sub = lax.broadcasted_iota(jnp.int32, (nsub, cw), 0)
    for j in range(S5_CHUNKS):
        cols = slice(j * cw, (j + 1) * cw)
        uj = up[:, j * uw:(j + 1) * uw]
        sr_ref[...] = jnp.dot(uj, bre_ref[j], preferred_element_type=F32)
        si_ref[...] = jnp.dot(uj, bim_ref[j], preferred_element_type=F32)
        ar = jnp.broadcast_to(are_ref[:, cols], (nsub, cw))
        ai = jnp.broadcast_to(aim_ref[:, cols], (nsub, cw))

        def step(t, carry, ar=ar, ai=ai):
            pr, pi = carry
            rows = pl.ds(pl.multiple_of(t * nsub, nsub), nsub)
            nr = ar * pr - ai * pi + sr_ref[rows, :]
            ni = ar * pi + ai * pr + si_ref[rows, :]
            sr_ref[rows, :] = nr
            si_ref[rows, :] = ni
            return nr, ni

        zero = jnp.zeros((nsub, cw), F32)
        loc_r, loc_i = lax.fori_loop(0, m, step, (zero, zero), unroll=4)

        er, ei = st_re[:, cols], st_im[:, cols]
        amr, ami = apr_ref[m - 1:m, cols], api_ref[m - 1:m, cols]
        ent_r, ent_i = zero, zero
        for q in range(nsub):
            ent_r = jnp.where(sub == q, er, ent_r)
            ent_i = jnp.where(sub == q, ei, ent_i)
            er, ei = (amr * er - ami * ei + loc_r[q:q + 1, :], amr * ei + ami * er + loc_i[q:q + 1, :])
        st_re[:, cols] = er
        st_im[:, cols] = ei

        def fix(t, carry, ent_r=ent_r, ent_i=ent_i, cols=cols):
            rows = pl.ds(pl.multiple_of(t * nsub, nsub), nsub)
            pr = apr_ref[pl.ds(t, 1), cols]
            pi = api_ref[pl.ds(t, 1), cols]
            sr_ref[rows, :] = sr_ref[rows, :] + (pr * ent_r - pi * ent_i)
            si_ref[rows, :] = si_ref[rows, :] + (pr * ent_i + pi * ent_r)
            return carry

        lax.fori_loop(0, m, fix, 0, unroll=4)
        y_ref[:, j * uw:(j + 1) * uw] = (
            jnp.dot(sr_ref[...].astype(BF16), cre_ref[j], preferred_element_type=F32)
            - jnp.dot(si_ref[...].astype(BF16), cim_ref[j], preferred_element_type=F32))

    y = _gelu_tanh(y_ref[...] + d_ref[...] * up.astype(F32))
    gate = jnp.dot(y.astype(BF16), wglu_ref[...], preferred_element_type=F32)
    out = y * jax.nn.sigmoid(gate)
    ms = jnp.mean(out * out, axis=-1, keepdims=True)
    outp = (out * lax.rsqrt(ms + EPS) * nw_ref[...]).astype(BF16)
    o_ref[...] = jnp.dot(permt_ref[...], outp, preferred_element_type=F32).astype(o_ref.dtype)


def _s5_discretise(lam_re, lam_im, log_dt, b_re, b_im):
    dt = jnp.exp(log_dt.astype(F32))[:, None]
    lr, li = lam_re.astype(F32), lam_im.astype(F32)
    mag = jnp.exp(lr * dt)
    ab_re = mag * jnp.cos(li * dt)
    ab_im = mag * jnp.sin(li * dt)
    den = lr * lr + li * li
    f_re = ((ab_re - 1.0) * lr + ab_im * li) / den
    f_im = (ab_im * lr - (ab_re - 1.0) * li) / den
    br, bi = b_re.astype(F32), b_im.astype(F32)
    bb_re = f_re[..., None] * br - f_im[..., None] * bi
    bb_im = f_re[..., None] * bi + f_im[..., None] * br
    return ab_re, ab_im, bb_re, bb_im


def _block_diag(blocks, per):
    g, r, c = blocks.shape
    b = blocks.reshape(g // per, per, r, c)
    eye = jnp.eye(per, dtype=blocks.dtype)
    return jnp.einsum("nirc,ij->nirjc", b, eye).reshape(g // per, per * r, per * c)


def s5_mixer(proj, bsz, seq, lam_re, lam_im, log_dt, b_re, b_im, c_re, c_im, d_skip, w_glu, norm_w, tc=256):
    per = S5_LANE_CHUNK // S5_STATE
    ab_re, ab_im, bb_re, bb_im = _s5_discretise(lam_re, lam_im, log_dt, b_re, b_im)
    bre = _block_diag(jnp.swapaxes(bb_re, 1, 2), per).astype(BF16)
    bim = _block_diag(jnp.swapaxes(bb_im, 1, 2), per).astype(BF16)
    cre = _block_diag(jnp.swapaxes(c_re.astype(F32), 1, 2), per).astype(BF16)
    cim = _block_diag(jnp.swapaxes(c_im.astype(F32), 1, 2), per).astype(BF16)
    nstate = ab_re.size
    tc = min(tc, seq)
    nct = seq // tc
    m = tc // SUBLANES
    steps = jnp.arange(1, m + 1, dtype=F32)[:, None, None]
    dt = jnp.exp(log_dt.astype(F32))[None, :, None]
    pow_mag = jnp.exp(steps * lam_re.astype(F32)[None] * dt)
    pow_ang = steps * lam_im.astype(F32)[None] * dt
    apow_re = (pow_mag * jnp.cos(pow_ang)).reshape(m, nstate)
    apow_im = (pow_mag * jnp.sin(pow_ang)).reshape(m, nstate)
    r = jnp.arange(tc)
    perm = (r[None, :] == ((r % SUBLANES) * m + r // SUBLANES)[:, None]).astype(BF16)
    uw = per * S5_CH_PER_GROUP
    const2 = lambda b, c: (0, 0)
    const3 = lambda b, c: (0, 0, 0)
    return pl.pallas_call(
        functools.partial(_s5_kernel, tc=tc),
        grid=(bsz, nct),
        in_specs=[
            pl.BlockSpec((tc, GROUP_WIDTH), lambda b, c: (b * nct + c, COL_S5)),
            pl.BlockSpec((tc, tc), const2),
            pl.BlockSpec((tc, tc), const2),
            pl.BlockSpec((S5_CHUNKS, uw, S5_LANE_CHUNK), const3),
            pl.BlockSpec((S5_CHUNKS, uw, S5_LANE_CHUNK), const3),
            pl.BlockSpec((S5_CHUNKS, S5_LANE_CHUNK, uw), const3),
            pl.BlockSpec((S5_CHUNKS, S5_LANE_CHUNK, uw), const3),
            pl.BlockSpec((1, nstate), const2),
            pl.BlockSpec((1, nstate), const2),
            pl.BlockSpec((m, nstate), const2),
            pl.BlockSpec((m, nstate), const2),
            pl.BlockSpec((1, GROUP_WIDTH), const2),
            pl.BlockSpec((GROUP_WIDTH, GROUP_WIDTH), const2),
            pl.BlockSpec((1, GROUP_WIDTH), const2),
        ],
        out_specs=pl.BlockSpec((tc, GROUP_WIDTH), lambda b, c: (b * nct + c, 0)),
        out_shape=jax.ShapeDtypeStruct((bsz * seq, GROUP_WIDTH), BF16),
        scratch_shapes=[
            pltpu.VMEM((tc, S5_LANE_CHUNK), F32),
            pltpu.VMEM((tc, S5_LANE_CHUNK), F32),
            pltpu.VMEM((tc, GROUP_WIDTH), F32),
            pltpu.VMEM((1, nstate), F32),
            pltpu.VMEM((1, nstate), F32),
        ],
        compiler_params=_cparams(("parallel", "arbitrary")),
        name="s5_mixer",
    )(proj, perm, perm.T, bre, bim, cre, cim, ab_re.reshape(1, nstate), ab_im.reshape(1, nstate),
      apow_re, apow_im, d_skip.reshape(1, GROUP_WIDTH).astype(F32), w_glu.astype(BF16), norm_w.reshape(1, GROUP_WIDTH).astype(F32))


def _rel_bucket(dist):
    n = jnp.maximum(dist, 0)
    max_exact = REL_BUCKETS // 2
    log_ratio = jnp.log(jnp.maximum(n, 1).astype(F32) / max_exact) / math.log(REL_MAX_DIST / max_exact)
    large = max_exact + (log_ratio * (REL_BUCKETS - max_exact)).astype(jnp.int32)
    large = jnp.minimum(large, REL_BUCKETS - 1)
    return jnp.where(n < max_exact, n, large)


def _bias_tiles(tbl, blk):
    assert blk >= REL_MAX_DIST
    i = jnp.arange(blk)[:, None]
    j = jnp.arange(blk)[None, :]
    buckets = jnp.stack([_rel_bucket(i - j), _rel_bucket(blk + i - j), _rel_bucket(jnp.full((blk, blk), 2 * blk))])
    onehot = (buckets[..., None] == jnp.arange(REL_BUCKETS)).astype(F32)
    tiles = jnp.einsum("tijk,hk->htij", onehot, tbl, precision=lax.Precision.HIGHEST) * LOG2E
    visible = jnp.stack([i >= j, jnp.ones((blk, blk), bool), jnp.ones((blk, blk), bool)])
    return jnp.where(visible[None], tiles, NEG_INF).astype(F32)


def _qk(q, k):
    return lax.dot_general(q, k, (((1,), (1,)), ((), ())), preferred_element_type=F32)


def _attn_kernel(q_ref, k_ref, v_ref, bias_ref, p0_ref, p1_ref, o_ref,
                 qs_ref, s_ref, mx_ref, mb_ref, ls_ref, acc_ref, kmean_ref, out_ref, *, mode, scale, heads, group):
    blk = ATT_BLOCK
    own = pl.program_id(1)
    lane = lax.broadcasted_iota(jnp.int32, (blk, LANES), 1)
    n_maps = 2 if mode == "diff" else 1
    nb = k_ref.shape[0] // blk
    reps = blk // LANES

    if mode == "moba":
        @pl.when(own == 0)
        def _():
            for h in range(heads):
                kf = k_ref[:, h * LANES:(h + 1) * LANES].astype(F32).reshape(nb, blk, LANES)
                kmean = jnp.sum(kf, axis=1) * (1.0 / blk)
                kmean_ref[h] = jnp.concatenate([kmean, jnp.zeros((GATE_ROWS - nb, LANES), F32)], axis=0)

    for g0 in range(0, heads, group):
        maps = [(h, mi) for h in range(g0, g0 + group) for mi in range(n_maps)]
        for h in range(g0, g0 + group):
            hs = slice(h * LANES, (h + 1) * LANES)
            q = q_ref[:, hs]
            if mode == "diff":
                qf = q.astype(F32)
                qs_ref[(h - g0) * 2] = jnp.where(lane < DIFF_HEAD_DIM, qf, 0.0).astype(BF16)
                qs_ref[(h - g0) * 2 + 1] = jnp.where(lane >= DIFF_HEAD_DIM, qf, 0.0).astype(BF16)
            else:
                hi, mid, lo = _split3(kmean_ref[h])
                gate = _qk(hi, q) + _qk(mid, q) + _qk(lo, q)
                sub = lax.broadcasted_iota(jnp.int32, (GATE_ROWS, blk), 0)
                gate = jnp.where(sub < own, gate, NEG_INF)
                penalty = jnp.zeros((GATE_ROWS, blk), F32)
                for n in range(nb):
                    row = gate[n:n + 1, :]
                    beats = jnp.logical_and(
                        jnp.logical_or(gate > row, jnp.logical_and(gate == row, sub < n)), sub < nb)
                    rank = jnp.sum(beats.astype(F32), axis=0, keepdims=True)
                    keep = jnp.logical_or(jnp.logical_and(rank < MOBA_TOPK, n < own), n == own)
                    penalty = jnp.where(sub == n, jnp.where(keep, 0.0, NEG_INF), penalty)
                pen_t = jnp.transpose(
                    jnp.concatenate([penalty, jnp.zeros((LANES - GATE_ROWS, blk), F32)], axis=0))
                qs_ref[h - g0] = jnp.concatenate([q, pen_t.astype(BF16)], axis=1)

        for idx in range(len(maps)):
            mx_ref[idx] = jnp.full((blk, LANES), -jnp.inf, F32)

        def logits_pass(n, carry, g0=g0, maps=maps):
            start = pl.multiple_of(n * blk, blk)
            tile_dist = jnp.minimum(own - n, 2)
            if mode == "moba":
                onehot = jnp.where(lane == n, 1.0, 0.0).astype(BF16)
            for idx, (h, mi) in enumerate(maps):
                hs = slice(h * LANES, (h + 1) * LANES)
                kn = k_ref[pl.ds(start, blk), hs]
                if mode == "moba":
                    kn = jnp.concatenate([kn, onehot], axis=1)
                s = _qk(qs_ref[idx], kn) * (scale * LOG2E) + bias_ref[h, tile_dist]
                s_ref[idx, n] = s
                part = s[:, :LANES]
                for r in range(1, reps):
                    part = jnp.maximum(part, s[:, r * LANES:(r + 1) * LANES])
                mx_ref[idx] = jnp.maximum(mx_ref[idx], part)
            return carry

        lax.fori_loop(0, own + 1, logits_pass, 0)

        for idx in range(len(maps)):
            mb_ref[idx] = jnp.broadcast_to(jnp.max(mx_ref[idx], axis=-1, keepdims=True), (blk, LANES))
            ls_ref[idx] = jnp.zeros((blk, LANES), F32)
            acc_ref[idx] = jnp.zeros((blk, LANES), F32)

        def value_pass(n, carry, maps=maps):
            start = pl.multiple_of(n * blk, blk)
            for idx, (h, mi) in enumerate(maps):
                hs = slice(h * LANES, (h + 1) * LANES)
                s = s_ref[idx, n]
                mb = mb_ref[idx]
                ps = [jnp.exp2(s[:, r * LANES:(r + 1) * LANES] - mb) for r in range(reps)]
                tot = ps[0]
                for r in range(1, reps):
                    tot = tot + ps[r]
                ls_ref[idx] += tot
                p = jnp.concatenate(ps, axis=1).astype(BF16)
                acc_ref[idx] += jnp.dot(p, v_ref[pl.ds(start, blk), hs], preferred_element_type=F32)
            return carry

        lax.fori_loop(0, own + 1, value_pass, 0)

        for h in range(g0, g0 + group):
            hs = slice(h * LANES, (h + 1) * LANES)
            i0 = (h - g0) * n_maps
            o = acc_ref[i0] / jnp.sum(ls_ref[i0], axis=-1, keepdims=True)
            if mode == "diff":
                o2 = acc_ref[i0 + 1] / jnp.sum(ls_ref[i0 + 1], axis=-1, keepdims=True)
                o = o - p0_ref[...] * o2
                ms = jnp.mean(o * o, axis=-1, keepdims=True)
                o_ref[:, hs] = (o * lax.rsqrt(ms + EPS) * p1_ref[...]).astype(o_ref.dtype)
            else:
                out_ref[:, hs] = o

    if mode == "moba":
        o = out_ref[...]
        ms = jnp.mean(o * o, axis=-1, keepdims=True)
        o_ref[...] = (o * lax.rsqrt(ms + EPS) * p1_ref[...]).astype(o_ref.dtype)


def attention(proj, bsz, seq, col_q, col_k, col_v, bias, p0, p1, mode, scale, heads):
    blk = ATT_BLOCK
    nq = seq // blk
    width = heads * LANES
    assert width == GROUP_WIDTH and seq % blk == 0
    nb = seq // blk
    assert nb <= GATE_ROWS
    n_maps = 2 if mode == "diff" else 1
    group = ATT_MAPS_PER_PASS // n_maps
    const2 = lambda b, i: (0, 0)
    return pl.pallas_call(
        functools.partial(_attn_kernel, mode=mode, scale=scale, heads=heads, group=group),
        grid=(bsz, nq),
        in_specs=[
            pl.BlockSpec((blk, width), lambda b, i: (b * nq + i, col_q)),
            pl.BlockSpec((seq, width), lambda b, i: (b, col_k)),
            pl.BlockSpec((seq, width), lambda b, i: (b, col_v)),
            pl.BlockSpec((heads, 3, blk, blk), lambda b, i: (0, 0, 0, 0)),
            pl.BlockSpec(p0.shape, const2),
            pl.BlockSpec(p1.shape, const2),
        ],
        out_specs=pl.BlockSpec((blk, width), lambda b, i: (b * nq + i, 0)),
        out_shape=jax.ShapeDtypeStruct((bsz * seq, width), BF16),
        scratch_shapes=[
            pltpu.VMEM((ATT_MAPS_PER_PASS, blk, LANES * (1 if mode == "diff" else 2)), BF16),
            pltpu.VMEM((ATT_MAPS_PER_PASS, nb, blk, blk), F32),
            pltpu.VMEM((ATT_MAPS_PER_PASS, blk, LANES), F32),
            pltpu.VMEM((ATT_MAPS_PER_PASS, blk, LANES), F32),
            pltpu.VMEM((ATT_MAPS_PER_PASS, blk, LANES), F32),
            pltpu.VMEM((ATT_MAPS_PER_PASS, blk, LANES), F32),
            pltpu.VMEM((heads, GATE_ROWS, LANES), F32),
            pltpu.VMEM((blk, width), F32),
        ],
        compiler_params=_cparams(("parallel", "arbitrary")),
        name="attn_" + mode,
    )(proj, proj, proj, bias, p0, p1)


def _softplus(x):
    return jnp.maximum(x, 0.0) + jnp.log(1.0 + jnp.exp(-jnp.abs(x)))


def _silu(x):
    return x * jax.nn.sigmoid(x)


def _ssd_kernel(z_ref, xbc_ref, dt_ref, dtt_ref, cw_ref, cb_ref, dtb_ref, dtbt_ref, a_ref, at_ref, dsk_ref,
                nw_ref, o_ref, ext_ref, st_ref, y_ref, *, lc):
    halo = 8
    c = pl.program_id(1)

    @pl.when(c == 0)
    def _():
        ext_ref[0:halo, :] = jnp.zeros((halo, SSD_XBC), F32)
        st_ref[...] = jnp.zeros_like(st_ref)

    ext_ref[halo:halo + lc, :] = xbc_ref[...].astype(F32)
    conv = cb_ref[...] + cw_ref[SSD_CONV - 1:SSD_CONV, :] * ext_ref[halo:halo + lc, :]
    for kk in range(1, SSD_CONV):
        conv = conv + cw_ref[SSD_CONV - 1 - kk:SSD_CONV - kk, :] * ext_ref[halo - kk:halo - kk + lc, :]
    ext_ref[0:halo, :] = ext_ref[lc:lc + halo, :]
    xbc = _silu(conv)
    xs = xbc[:, :GROUP_WIDTH]

    dt_c = _softplus(dt_ref[...] + dtb_ref[...])
    dt_r = _softplus(dtt_ref[...] + dtbt_ref[...])
    a_c = dt_c * a_ref[...]
    a_r = dt_r * at_ref[...]
    row = lax.broadcasted_iota(jnp.int32, (lc, lc), 0)
    colm = lax.broadcasted_iota(jnp.int32, (lc, lc), 1)
    causal = row >= colm
    tri = causal.astype(BF16)
    tri_t = (colm >= row).astype(BF16)
    acum_c = _dot_exact_rhs(tri, a_c)
    acum_r = _dot_exact_lhs(a_r, tri_t)
    alast_c = acum_c[lc - 1:lc, :]

    lane = lax.broadcasted_iota(jnp.int32, (lc, LANES), 1)
    lo_half = lane < SSD_HEAD_DIM
    heads_per_group = SSD_HEADS // SSD_GROUPS
    for g in range(SSD_GROUPS):
        bg = xbc[:, GROUP_WIDTH + g * SSD_STATE:GROUP_WIDTH + (g + 1) * SSD_STATE].astype(BF16)
        cg = xbc[:, GROUP_WIDTH + SSD_BC + g * SSD_STATE:GROUP_WIDTH + SSD_BC + (g + 1) * SSD_STATE].astype(BF16)
        cb = _qk(cg, bg)
        for pr in range(heads_per_group // 2):
            ha = g * heads_per_group + 2 * pr
            hb = ha + 1
            tile = slice(ha * SSD_HEAD_DIM, (ha + 2) * SSD_HEAD_DIM)
            xp = xs[:, tile]

            def per_head(col_a, col_b):
                return jnp.where(lo_half, col_a, col_b)

            xdt = xp * per_head(dt_c[:, ha:ha + 1], dt_c[:, hb:hb + 1])
            ydiag = None
            for hh, keep in ((ha, lo_half), (hb, jnp.logical_not(lo_half))):
                seg = acum_c[:, hh:hh + 1] - acum_r[hh:hh + 1, :]
                decay = jnp.where(causal, jnp.exp(jnp.where(causal, seg, 0.0)), 0.0)
                mm = (cb * decay).astype(BF16)
                part = jnp.dot(mm, jnp.where(keep, xdt, 0.0).astype(BF16), preferred_element_type=F32)
                ydiag = part if ydiag is None else ydiag + part
            st = st_ref[ha // 2]
            yoff = _qk(cg, st.astype(BF16)) * per_head(jnp.exp(acum_c[:, ha:ha + 1]), jnp.exp(acum_c[:, hb:hb + 1]))
            y_ref[:, tile] = ydiag + yoff + dsk_ref[:, tile] * xp
            to_end = per_head(jnp.exp(alast_c[:, ha:ha + 1] - acum_c[:, ha:ha + 1]),
                              jnp.exp(alast_c[:, hb:hb + 1] - acum_c[:, hb:hb + 1]))
            xdec_t = jnp.transpose(xdt * to_end).astype(BF16)
            sub = lax.broadcasted_iota(jnp.int32, (LANES, SSD_STATE), 0)
            chunk_decay = jnp.where(sub < SSD_HEAD_DIM, jnp.exp(alast_c[:, ha:ha + 1]),
                                    jnp.exp(alast_c[:, hb:hb + 1]))
            st_ref[ha // 2] = st * chunk_decay + jnp.dot(xdec_t, bg, preferred_element_type=F32)

    y = y_ref[...] * _silu(z_ref[...].astype(F32))
    gw = GROUP_WIDTH // SSD_GROUPS
    for g in range(SSD_GROUPS):
        yg = y[:, g * gw:(g + 1) * gw]
        ms = jnp.mean(yg * yg, axis=-1, keepdims=True)
        o_ref[:, g * gw:(g + 1) * gw] = (yg * lax.rsqrt(ms + EPS) * nw_ref[:, g * gw:(g + 1) * gw]).astype(o_ref.dtype)


def _pad_lanes(v):
    return jnp.pad(v.astype(F32), (0, LANES - v.shape[0])).reshape(1, LANES)


def ssd_mixer(proj, dt_raw, bsz, seq, conv_w, conv_b, dt_bias, a_log, d_skip, norm_w, lc=128):
    nc = seq // lc
    t = bsz * seq
    a = -jnp.exp(a_log.astype(F32))
    dt_t = jnp.transpose(dt_raw[:, :SSD_HEADS])
    col16 = lambda v: jnp.broadcast_to(v.astype(F32)[:, None], (SSD_HEADS, LANES))
    dskip = jnp.repeat(d_skip.astype(F32), SSD_HEAD_DIM).reshape(1, GROUP_WIDTH)
    const2 = lambda b, c: (0, 0)
    return pl.pallas_call(
        functools.partial(_ssd_kernel, lc=lc),
        grid=(bsz, nc),
        in_specs=[
            pl.BlockSpec((lc, GROUP_WIDTH), lambda b, c: (b * nc + c, COL_Z)),
            pl.BlockSpec((lc, SSD_XBC), lambda b, c: (b * nc + c, COL_XBC)),
            pl.BlockSpec((lc, LANES), lambda b, c: (b * nc + c, 0)),
            pl.BlockSpec((SSD_HEADS, lc), lambda b, c: (0, b * nc + c)),
            pl.BlockSpec((SSD_CONV, SSD_XBC), const2),
            pl.BlockSpec((1, SSD_XBC), const2),
            pl.BlockSpec((1, LANES), const2),
            pl.BlockSpec((SSD_HEADS, LANES), const2),
            pl.BlockSpec((1, LANES), const2),
            pl.BlockSpec((SSD_HEADS, LANES), const2),
            pl.BlockSpec((1, GROUP_WIDTH), const2),
            pl.BlockSpec((1, GROUP_WIDTH), const2),
        ],
        out_specs=pl.BlockSpec((lc, GROUP_WIDTH), lambda b, c: (b * nc + c, 0)),
        out_shape=jax.ShapeDtypeStruct((t, GROUP_WIDTH), BF16),
        scratch_shapes=[
            pltpu.VMEM((lc + 8, SSD_XBC), F32),
            pltpu.VMEM((SSD_HEADS // 2, 2 * SSD_HEAD_DIM, SSD_STATE), F32),
            pltpu.VMEM((lc, GROUP_WIDTH), F32),
        ],
        compiler_params=_cparams(("parallel", "arbitrary")),
        name="ssd_mixer",
    )(proj, proj, dt_raw, dt_t, conv_w.reshape(SSD_CONV, SSD_XBC).astype(F32),
      conv_b.reshape(1, SSD_XBC).astype(F32), _pad_lanes(dt_bias), col16(dt_bias), _pad_lanes(a), col16(a),
      dskip, norm_w.reshape(1, GROUP_WIDTH).astype(F32))


def kernel(x, rel_bias_table, attn_norm_w, w_in, s5_lam_re, s5_lam_im, s5_log_dt, s5_b_re, s5_b_im, s5_c_re, s5_c_im, s5_d, s5_w_glu, s5_out_norm_w, diff_lam_q1, diff_lam_k1, diff_lam_q2, diff_lam_k2, diff_subln_w, moba_out_norm_w, ssd_conv_w, ssd_conv_b, ssd_dt_bias, ssd_a_log, ssd_d, ssd_norm_w, w_out, mlp_norm_w, w_up, w_down, final_norm_w):
    bsz, seq, d_model = x.shape
    depth = w_in.shape[0]
    t = bsz * seq
    x = x.reshape(t, d_model).astype(F32)

    w_in_t = jnp.swapaxes(w_in, 1, 2)
    w_dt_b = jnp.pad(w_in[:, :, PROJ_MAIN:], ((0, 0), (0, 0), (0, LANES - SSD_HEADS))).astype(BF16)

    tbl = rel_bias_table.astype(F32).T
    bias_diff = _bias_tiles(tbl[:DIFF_HEADS], ATT_BLOCK)
    bias_moba = _bias_tiles(tbl[DIFF_HEADS:], ATT_BLOCK)

    for l in range(depth):
        h = rmsnorm(x, attn_norm_w[l], BF16)
        proj = matmul_wstat([h], w_in_t, l, PROJ_MAIN, BF16, w_transposed=True)
        dt_raw = matmul([h], w_dt_b, l, LANES, F32)

        y_s5 = s5_mixer(proj, bsz, seq, s5_lam_re[l], s5_lam_im[l], s5_log_dt[l], s5_b_re[l], s5_b_im[l],
                        s5_c_re[l], s5_c_im[l], s5_d[l], s5_w_glu[l], s5_out_norm_w[l])

        lam_init = 0.8 - 0.6 * math.exp(-0.3 * l)
        lam = (jnp.exp(jnp.sum(diff_lam_q1[l].astype(F32) * diff_lam_k1[l].astype(F32)))
               - jnp.exp(jnp.sum(diff_lam_q2[l].astype(F32) * diff_lam_k2[l].astype(F32))) + lam_init)
        y_diff = attention(proj, bsz, seq, COL_DQ, COL_DK, COL_DV, bias_diff,
                           jnp.broadcast_to(lam, (1, LANES)).astype(F32),
                           (diff_subln_w[l].astype(F32) * (1.0 - lam_init)).reshape(1, LANES),
                           "diff", DIFF_HEAD_DIM ** -0.5, DIFF_HEADS)
        y_moba = attention(proj, bsz, seq, COL_MQ, COL_MK, COL_MV, bias_moba,
                           jnp.zeros((1, LANES), F32),
                           moba_out_norm_w[l].astype(F32).reshape(1, GROUP_WIDTH),
                           "moba", (GROUP_WIDTH // MOBA_HEADS) ** -0.5, MOBA_HEADS)
        y_ssd = ssd_mixer(proj, dt_raw, bsz, seq, ssd_conv_w[l], ssd_conv_b[l], ssd_dt_bias[l], ssd_a_log[l],
                          ssd_d[l], ssd_norm_w[l])

        x = matmul_wstat([y_s5, y_diff, y_moba, y_ssd], w_out, l, d_model, F32, residual=x)
        h = rmsnorm(x, mlp_norm_w[l], BF16)
        u, w_down_b = matmul_wstat([h], w_up, l, w_up.shape[2], BF16, act="relu2", cast_through=w_down)
        x = matmul([u], w_down_b[None], 0, d_model, F32, residual=x)

    return rmsnorm(x, final_norm_w, F32).reshape(bsz, seq, d_model)
```

```python
import functools
import math

import jax
import jax.numpy as jnp
from jax import lax
from jax.experimental import pallas as pl
from jax.experimental.pallas import tpu as pltpu

F32 = jnp.float32
BF16 = jnp.bfloat16
EPS = 1e-6
NEG_INF = -1e30

GROUP_WIDTH = 1024
S5_CH_PER_GROUP = 16
S5_STATE = 64
DIFF_HEADS = 8
DIFF_HEAD_DIM = 64
MOBA_HEADS = 8
MOBA_BLOCK = 256
MOBA_TOPK = 3
SSD_HEAD_DIM = 64
SSD_HEADS = 16
SSD_GROUPS = 4
SSD_STATE = 128
SSD_CONV = 4
SSD_BC = SSD_GROUPS * SSD_STATE
SSD_XBC = GROUP_WIDTH + 2 * SSD_BC
REL_BUCKETS = 32
REL_MAX_DIST = 128

COL_S5 = 0
COL_DQ, COL_DK, COL_DV = 1, 2, 3
COL_MQ, COL_MK, COL_MV = 4, 5, 6
COL_Z = 7
COL_XBC = 4
PROJ_MAIN = 10 * GROUP_WIDTH

LANES = 128
SUBLANES = 8
ATT_BLOCK = 256
ATT_MAPS_PER_PASS = 8
GATE_ROWS = 16
LOG2E = math.log2(math.e)
S5_LANE_CHUNK = 512
S5_CHUNKS = (GROUP_WIDTH // S5_CH_PER_GROUP) * S5_STATE // S5_LANE_CHUNK
WSTAT_PIECES = 4
VMEM_LIMIT = 56 * 1024 * 1024


def _cparams(sem):
    return pltpu.CompilerParams(dimension_semantics=sem, vmem_limit_bytes=VMEM_LIMIT)


def _rmsnorm_kernel(x_ref, w_ref, o_ref):
    x = x_ref[...].astype(F32)
    ms = jnp.mean(x * x, axis=-1, keepdims=True)
    o_ref[...] = (x * lax.rsqrt(ms + EPS) * w_ref[...]).astype(o_ref.dtype)


def rmsnorm(x, w, out_dtype, tm=256):
    t, d = x.shape
    return pl.pallas_call(
        _rmsnorm_kernel,
        grid=(t // tm,),
        in_specs=[pl.BlockSpec((tm, d), lambda i: (i, 0)), pl.BlockSpec((1, d), lambda i: (0, 0))],
        out_specs=pl.BlockSpec((tm, d), lambda i: (i, 0)),
        out_shape=jax.ShapeDtypeStruct((t, d), out_dtype),
        compiler_params=_cparams(("parallel",)),
        name="rmsnorm",
    )(x, w.reshape(1, d).astype(F32))


def _split_refs(refs, n_lhs, n_w, has_res, has_scale, has_norm, has_cast):
    it = iter(refs)
    take = lambda n: [next(it) for _ in range(n)]
    r = {"lhs": take(n_lhs), "w": take(n_w)}
    r["res"] = next(it) if has_res else None
    r["scale"] = next(it) if has_scale else None
    r["normw"] = next(it) if has_norm else None
    r["cast_in"] = next(it) if has_cast else None
    r["out"] = next(it)
    r["xw"], r["ssq"] = (next(it), next(it)) if has_norm else (None, None)
    r["cast_out"] = next(it) if has_cast else None
    r["scratch"] = list(it)
    return r


def _mm_epilogue(acc, r, act):
    if r["scale"] is not None:
        acc = acc * r["scale"][...]
    if act == "relu2":
        p = jnp.maximum(acc, 0.0)
        acc = p * p
    if r["res"] is not None:
        acc = r["res"][...] + acc
    r["out"][...] = acc.astype(r["out"].dtype)
    if r["normw"] is not None:
        r["xw"][...] = (acc * r["normw"][...]).astype(BF16)
        r["ssq"][...] = jnp.sum(acc * acc, axis=-1, keepdims=True)


def _mm_kernel(*refs, nk, act, flags):
    r = _split_refs(refs, 1, 1, *flags)
    a_ref, w_ref, acc_ref = r["lhs"][0], r["w"][0], r["scratch"][0]
    k = pl.program_id(2)

    def partial_dot():
        return jnp.dot(a_ref[...], w_ref[...], preferred_element_type=F32)

    if nk == 1:
        _mm_epilogue(partial_dot(), r, act)
        return

    @pl.when(k == 0)
    def _():
        acc_ref[...] = partial_dot()

    @pl.when(jnp.logical_and(k > 0, k < nk - 1))
    def _():
        acc_ref[...] += partial_dot()

    @pl.when(k == nk - 1)
    def _():
        _mm_epilogue(acc_ref[...] + partial_dot(), r, act)


def _norm_out(m, n_out, tm, tn, nj, idx, idx3):
    specs = [pl.BlockSpec((tm, tn), idx), pl.BlockSpec((None, tm, 1), idx3)]
    shapes = [jax.ShapeDtypeStruct((m, n_out), BF16), jax.ShapeDtypeStruct((nj, m, 1), F32)]
    return specs, shapes


def matmul(lhs, w3, layer, n_out, out_dtype, act=None, residual=None, row_scale=None, norm_w=None,
           tm=1024, tn=1024, tk=2048):
    m, kdim = lhs.shape[0], w3.shape[1]
    tm, tn, tk = min(tm, m), min(tn, n_out), min(tk, kdim)
    nk = kdim // tk
    assert m % tm == 0 and n_out % tn == 0 and kdim % tk == 0
    tile = lambda i, j, k: (i, j)
    in_specs = [pl.BlockSpec((tm, tk), lambda i, j, k: (i, k)),
                pl.BlockSpec((None, tk, tn), lambda i, j, k: (layer, k, j))]
    args = [lhs, w3]
    if residual is not None:
        in_specs.append(pl.BlockSpec((tm, tn), tile))
        args.append(residual)
    if row_scale is not None:
        in_specs.append(pl.BlockSpec((tm, 1), lambda i, j, k: (i, 0)))
        args.append(row_scale)
    out_specs = [pl.BlockSpec((tm, tn), tile)]
    out_shape = [jax.ShapeDtypeStruct((m, n_out), out_dtype)]
    if norm_w is not None:
        in_specs.append(pl.BlockSpec((1, tn), lambda i, j, k: (0, j)))
        args.append(norm_w.reshape(1, n_out).astype(F32))
        specs, shapes = _norm_out(m, n_out, tm, tn, n_out // tn, tile, lambda i, j, k: (j, i, 0))
        out_specs += specs
        out_shape += shapes
    flags = (residual is not None, row_scale is not None, norm_w is not None, False)
    res = pl.pallas_call(
        functools.partial(_mm_kernel, nk=nk, act=act, flags=flags),
        grid=(m // tm, n_out // tn, nk),
        in_specs=in_specs,
        out_specs=out_specs,
        out_shape=out_shape,
        scratch_shapes=[pltpu.VMEM((tm, tn) if nk > 1 else (8, LANES), F32)],
        compiler_params=_cparams(("parallel", "parallel", "arbitrary")),
        name="matmul_" + (act or "lin") + ("_res" if residual is not None else ""),
    )(*args)
    return res if norm_w is not None else res[0]


def _mm_wstat_kernel(*refs, act, n_lhs, flags, w_transposed):
    r = _split_refs(refs, n_lhs, WSTAT_PIECES, *flags)
    wb_ref = r["scratch"][0]
    kp = r["w"][0].shape[1 if w_transposed else 0]

    @pl.when(pl.program_id(1) == 0)
    def _():
        for p, w_ref in enumerate(r["w"]):
            if w_transposed:
                wb_ref[:, p * kp:(p + 1) * kp] = w_ref[...].astype(BF16)
            else:
                wb_ref[p * kp:(p + 1) * kp, :] = w_ref[...].astype(BF16)

    if r["cast_in"] is not None:
        r["cast_out"][...] = r["cast_in"][...].astype(BF16)

    kw = r["lhs"][0].shape[1]
    acc = None
    for i, a_ref in enumerate(r["lhs"]):
        if w_transposed:
            p = lax.dot_general(a_ref[...], wb_ref[:, i * kw:(i + 1) * kw], (((1,), (1,)), ((), ())),
                                preferred_element_type=F32)
        else:
            p = jnp.dot(a_ref[...], wb_ref[i * kw:(i + 1) * kw, :], preferred_element_type=F32)
        acc = p if acc is None else acc + p
    _mm_epilogue(acc, r, act)


def matmul_wstat(lhs, w3, layer, n_out, out_dtype, act=None, residual=None, row_scale=None, norm_w=None,
                 w_transposed=False, cast_through=None, tm=1024, tn=512):
    lhs = list(lhs)
    m = lhs[0].shape[0]
    kdim = w3.shape[2 if w_transposed else 1]
    assert sum(a.shape[1] for a in lhs) == kdim and all(a.shape[1] == lhs[0].shape[1] for a in lhs)
    assert m % tm == 0 and n_out % tn == 0 and kdim % WSTAT_PIECES == 0
    ni, nj, kp = m // tm, n_out // tn, kdim // WSTAT_PIECES
    assert ni > WSTAT_PIECES
    tile = lambda j, i: (i, j)
    in_specs = [pl.BlockSpec((tm, a.shape[1]), lambda j, i: (i, 0)) for a in lhs]
    for p in range(WSTAT_PIECES):
        col = lambda j, i, p=p: jnp.minimum(j + (i > p).astype(jnp.int32), nj - 1)
        if w_transposed:
            in_specs.append(pl.BlockSpec((None, tn, kp), lambda j, i, p=p, col=col: (layer, col(j, i), p)))
        else:
            in_specs.append(pl.BlockSpec((None, kp, tn), lambda j, i, p=p, col=col: (layer, p, col(j, i))))
    args = lhs + [w3] * WSTAT_PIECES
    if residual is not None:
        in_specs.append(pl.BlockSpec((tm, tn), tile))
        args.append(residual)
    if row_scale is not None:
        in_specs.append(pl.BlockSpec((tm, 1), lambda j, i: (i, 0)))
        args.append(row_scale)
    if norm_w is not None:
        in_specs.append(pl.BlockSpec((1, tn), lambda j, i: (0, j)))
        args.append(norm_w.reshape(1, n_out).astype(F32))
    out_specs = [pl.BlockSpec((tm, tn), tile)]
    out_shape = [jax.ShapeDtypeStruct((m, n_out), out_dtype)]
    if norm_w is not None:
        specs, shapes = _norm_out(m, n_out, tm, tn, nj, tile, lambda j, i: (j, i, 0))
        out_specs += specs
        out_shape += shapes
    if cast_through is not None:
        steps = nj * ni
        rows, cols = cast_through.shape[1:]
        assert rows % steps == 0
        slab = rows // steps
        in_specs.append(pl.BlockSpec((None, slab, cols), lambda j, i: (layer, j * ni + i, 0)))
        args.append(cast_through)
        out_specs.append(pl.BlockSpec((slab, cols), lambda j, i: (j * ni + i, 0)))
        out_shape.append(jax.ShapeDtypeStruct((rows, cols), BF16))
    flags = (residual is not None, row_scale is not None, norm_w is not None, cast_through is not None)
    res = pl.pallas_call(
        functools.partial(_mm_wstat_kernel, act=act, n_lhs=len(lhs), flags=flags, w_transposed=w_transposed),
        grid=(nj, ni),
        in_specs=in_specs,
        out_specs=out_specs,
        out_shape=out_shape,
        scratch_shapes=[pltpu.VMEM((tn, kdim) if w_transposed else (kdim, tn), BF16)],
        compiler_params=_cparams(("arbitrary", "arbitrary")),
        name="matmul_wstat_" + (act or "lin") + ("_res" if residual is not None else ""),
    )(*args)
    return res if len(res) > 1 else res[0]


def _split3(x):
    hi = x.astype(BF16)
    r1 = x - hi.astype(F32)
    mid = r1.astype(BF16)
    lo = (r1 - mid.astype(F32)).astype(BF16)
    return hi, mid, lo


def _dot_exact_lhs(x, sel):
    hi, mid, lo = _split3(x)
    d = functools.partial(jnp.dot, preferred_element_type=F32)
    return d(hi, sel) + d(mid, sel) + d(lo, sel)


def _dot_exact_rhs(sel, x):
    hi, mid, lo = _split3(x)
    d = functools.partial(jnp.dot, preferred_element_type=F32)
    return d(sel, hi) + d(sel, mid) + d(sel, lo)


def _gelu_tanh(x):
    c = math.sqrt(2.0 / math.pi)
    return 0.5 * x * (1.0 + jnp.tanh(c * (x + 0.044715 * (x * x * x))))


def _s5_kernel(u_ref, perm_ref, permt_ref, bre_ref, bim_ref, cre_ref, cim_ref, are_ref, aim_ref,
               apr_ref, api_ref, d_ref, wglu_ref, nw_ref, o_ref, sr_ref, si_ref, y_ref, st_re, st_im, *, tc):
    nsub = SUBLANES
    m = tc // nsub

    @pl.when(pl.program_id(1) == 0)
    def _():
        st_re[...] = jnp.zeros_like(st_re)
        st_im[...] = jnp.zeros_like(st_im)

    up = jnp.dot(perm_ref[...], u_ref[...], preferred_element_type=F32).astype(BF16)

    cw = S5_LANE_CHUNK
    uw = cw // S5_STATE * S5_CH_PER_GROUP
    sub = lax.broadcasted_iota(jnp.int32, (nsub, cw), 0)
    for j in range(S5_CHUNKS):
        cols = slice(j * cw, (j + 1) * cw)
        uj = up[:, j * uw:(j + 1) * uw]
        sr_ref[...] = jnp.dot(uj, bre_ref[j], preferred_element_type=F32)
        si_ref[...] = jnp.dot(uj, bim_ref[j], preferred_element_type=F32)
        ar = jnp.broadcast_to(are_ref[:, cols], (nsub, cw))
        ai = jnp.broadcast_to(aim_ref[:, cols], (nsub, cw))

        def step(t, carry, ar=ar, ai=ai):
            pr, pi = carry
            rows = pl.ds(pl.multiple_of(t * nsub, nsub), nsub)
            nr = ar * pr - ai * pi + sr_ref[rows, :]
            ni = ar * pi + ai * pr + si_ref[rows, :]
            sr_ref[rows, :] = nr
            si_ref[rows, :] = ni
            return nr, ni

        zero = jnp.zeros((nsub, cw), F32)
        loc_r, loc_i = lax.fori_loop(0, m, step, (zero, zero), unroll=4)

        er, ei = st_re[:, cols], st_im[:, cols]
        amr, ami = apr_ref[m - 1:m, cols], api_ref[m - 1:m, cols]
        ent_r, ent_i = zero, zero
        for q in range(nsub):
            ent_r = jnp.where(sub == q, er, ent_r)
            ent_i = jnp.where(sub == q, ei, ent_i)
            er, ei = (amr * er - ami * ei + loc_r[q:q + 1, :], amr * ei + ami * er + loc_i[q:q + 1, :])
        st_re[:, cols] = er
        st_im[:, cols] = ei

        def fix(t, carry, ent_r=ent_r, ent_i=ent_i, cols=cols):
            rows = pl.ds(pl.multiple_of(t * nsub, nsub), nsub)
            pr = apr_ref[pl.ds(t, 1), cols]
            pi = api_ref[pl.ds(t, 1), cols]
            sr_ref[rows, :] = sr_ref[rows, :] + (pr * ent_r - pi * ent_i)
            si_ref[rows, :] = si_ref[rows, :] + (pr * ent_i + pi * ent_r)
            return carry

        lax.fori_loop(0, m, fix, 0, unroll=4)
        y_ref[:, j * uw:(j + 1) * uw] = (
            jnp.dot(sr_ref[...].astype(BF16), cre_ref[j], preferred_element_type=F32)
            - jnp.dot(si_ref[...].astype(BF16), cim_ref[j], preferred_element_type=F32))

    y = _gelu_tanh(y_ref[...] + d_ref[...] * up.astype(F32))
    gate = jnp.dot(y.astype(BF16), wglu_ref[...], preferred_element_type=F32)
    out = y * jax.nn.sigmoid(gate)
    ms = jnp.mean(out * out, axis=-1, keepdims=True)
    outp = (out * lax.rsqrt(ms + EPS) * nw_ref[...]).astype(BF16)
    o_ref[...] = jnp.dot(permt_ref[...], outp, preferred_element_type=F32).astype(o_ref.dtype)


def _s5_discretise(lam_re, lam_im, log_dt, b_re, b_im):
    dt = jnp.exp(log_dt.astype(F32))[:, None]
    lr, li = lam_re.astype(F32), lam_im.astype(F32)
    mag = jnp.exp(lr * dt)
    ab_re = mag * jnp.cos(li * dt)
    ab_im = mag * jnp.sin(li * dt)
    den = lr * lr + li * li
    f_re = ((ab_re - 1.0) * lr + ab_im * li) / den
    f_im = (ab_im * lr - (ab_re - 1.0) * li) / den
    br, bi = b_re.astype(F32), b_im.astype(F32)
    bb_re = f_re[..., None] * br - f_im[..., None] * bi
    bb_im = f_re[..., None] * bi + f_im[..., None] * br
    return ab_re, ab_im, bb_re, bb_im


def _block_diag(blocks, per):
    g, r, c = blocks.shape
    b = blocks.reshape(g // per, per, r, c)
    eye = jnp.eye(per, dtype=blocks.dtype)
    return jnp.einsum("nirc,ij->nirjc", b, eye).reshape(g // per, per * r, per * c)


def s5_mixer(proj, bsz, seq, lam_re, lam_im, log_dt, b_re, b_im, c_re, c_im, d_skip, w_glu, norm_w, tc=256):
    per = S5_LANE_CHUNK // S5_STATE
    ab_re, ab_im, bb_re, bb_im = _s5_discretise(lam_re, lam_im, log_dt, b_re, b_im)
    bre = _block_diag(jnp.swapaxes(bb_re, 1, 2), per).astype(BF16)
    bim = _block_diag(jnp.swapaxes(bb_im, 1, 2), per).astype(BF16)
    cre = _block_diag(jnp.swapaxes(c_re.astype(F32), 1, 2), per).astype(BF16)
    cim = _block_diag(jnp.swapaxes(c_im.astype(F32), 1, 2), per).astype(BF16)
    nstate = ab_re.size
    tc = min(tc, seq)
    nct = seq // tc
    m = tc // SUBLANES
    steps = jnp.arange(1, m + 1, dtype=F32)[:, None, None]
    dt = jnp.exp(log_dt.astype(F32))[None, :, None]
    pow_mag = jnp.exp(steps * lam_re.astype(F32)[None] * dt)
    pow_ang = steps * lam_im.astype(F32)[None] * dt
    apow_re = (pow_mag * jnp.cos(pow_ang)).reshape(m, nstate)
    apow_im = (pow_mag * jnp.sin(pow_ang)).reshape(m, nstate)
    r = jnp.arange(tc)
    perm = (r[None, :] == ((r % SUBLANES) * m + r // SUBLANES)[:, None]).astype(BF16)
    uw = per * S5_CH_PER_GROUP
    const2 = lambda b, c: (0, 0)
    const3 = lambda b, c: (0, 0, 0)
    return pl.pallas_call(
        functools.partial(_s5_kernel, tc=tc),
        grid=(bsz, nct),
        in_specs=[
            pl.BlockSpec((tc, GROUP_WIDTH), lambda b, c: (b * nct + c, COL_S5)),
            pl.BlockSpec((tc, tc), const2),
            pl.BlockSpec((tc, tc), const2),
            pl.BlockSpec((S5_CHUNKS, uw, S5_LANE_CHUNK), const3),
            pl.BlockSpec((S5_CHUNKS, uw, S5_LANE_CHUNK), const3),
            pl.BlockSpec((S5_CHUNKS, S5_LANE_CHUNK, uw), const3),
            pl.BlockSpec((S5_CHUNKS, S5_LANE_CHUNK, uw), const3),
            pl.BlockSpec((1, nstate), const2),
            pl.BlockSpec((1, nstate), const2),
            pl.BlockSpec((m, nstate), const2),
            pl.BlockSpec((m, nstate), const2),
            pl.BlockSpec((1, GROUP_WIDTH), const2),
            pl.BlockSpec((GROUP_WIDTH, GROUP_WIDTH), const2),
            pl.BlockSpec((1, GROUP_WIDTH), const2),
        ],
        out_specs=pl.BlockSpec((tc, GROUP_WIDTH), lambda b, c: (b * nct + c, 0)),
        out_shape=jax.ShapeDtypeStruct((bsz * seq, GROUP_WIDTH), BF16),
        scratch_shapes=[
            pltpu.VMEM((tc, S5_LANE_CHUNK), F32),
            pltpu.VMEM((tc, S5_LANE_CHUNK), F32),
            pltpu.VMEM((tc, GROUP_WIDTH), F32),
            pltpu.VMEM((1, nstate), F32),
            pltpu.VMEM((1, nstate), F32),
        ],
        compiler_params=_cparams(("parallel", "arbitrary")),
        name="s5_mixer",
    )(proj, perm, perm.T, bre, bim, cre, cim, ab_re.reshape(1, nstate), ab_im.reshape(1, nstate),
      apow_re, apow_im, d_skip.reshape(1, GROUP_WIDTH).astype(F32), w_glu.astype(BF16), norm_w.reshape(1, GROUP_WIDTH).astype(F32))


def _rel_bucket(dist):
    n = jnp.maximum(dist, 0)
    max_exact = REL_BUCKETS // 2
    log_ratio = jnp.log(jnp.maximum(n, 1).astype(F32) / max_exact) / math.log(REL_MAX_DIST / max_exact)
    large = max_exact + (log_ratio * (REL_BUCKETS - max_exact)).astype(jnp.int32)
    large = jnp.minimum(large, REL_BUCKETS - 1)
    return jnp.where(n < max_exact, n, large)


def _bias_tiles(tbl, blk):
    assert blk >= REL_MAX_DIST
    i = jnp.arange(blk)[:, None]
    j = jnp.arange(blk)[None, :]
    buckets = jnp.stack([_rel_bucket(i - j), _rel_bucket(blk + i - j), _rel_bucket(jnp.full((blk, blk), 2 * blk))])
    onehot = (buckets[..., None] == jnp.arange(REL_BUCKETS)).astype(F32)
    tiles = jnp.einsum("tijk,hk->htij", onehot, tbl, precision=lax.Precision.HIGHEST) * LOG2E
    visible = jnp.stack([i >= j, jnp.ones((blk, blk), bool), jnp.ones((blk, blk), bool)])
    return jnp.where(visible[None], tiles, NEG_INF).astype(F32)


def _qk(q, k):
    return lax.dot_general(q, k, (((1,), (1,)), ((), ())), preferred_element_type=F32)


def _attn_kernel(q_ref, k_ref, v_ref, bias_ref, p0_ref, p1_ref, o_ref,
                 qs_ref, s_ref, mx_ref, mb_ref, ls_ref, acc_ref, kmean_ref, out_ref, *, mode, scale, heads, group):
    blk = ATT_BLOCK
    own = pl.program_id(1)
    lane = lax.broadcasted_iota(jnp.int32, (blk, LANES), 1)
    n_maps = 2 if mode == "diff" else 1
    nb = k_ref.shape[0] // blk
    reps = blk // LANES

    if mode == "moba":
        @pl.when(own == 0)
        def _():
            for h in range(heads):
                kf = k_ref[:, h * LANES:(h + 1) * LANES].astype(F32).reshape(nb, blk, LANES)
                kmean = jnp.sum(kf, axis=1) * (1.0 / blk)
                kmean_ref[h] = jnp.concatenate([kmean, jnp.zeros((GATE_ROWS - nb, LANES), F32)], axis=0)

    for g0 in range(0, heads, group):
        maps = [(h, mi) for h in range(g0, g0 + group) for mi in range(n_maps)]
        for h in range(g0, g0 + group):
            hs = slice(h * LANES, (h + 1) * LANES)
            q = q_ref[:, hs]
            if mode == "diff":
                qf = q.astype(F32)
                qs_ref[(h - g0) * 2] = jnp.where(lane < DIFF_HEAD_DIM, qf, 0.0).astype(BF16)
                qs_ref[(h - g0) * 2 + 1] = jnp.where(lane >= DIFF_HEAD_DIM, qf, 0.0).astype(BF16)
            else:
                hi, mid, lo = _split3(kmean_ref[h])
                gate = _qk(hi, q) + _qk(mid, q) + _qk(lo, q)
                sub = lax.broadcasted_iota(jnp.int32, (GATE_ROWS, blk), 0)
                gate = jnp.where(sub < own, gate, NEG_INF)
                penalty = jnp.zeros((GATE_ROWS, blk), F32)
                for n in range(nb):
                    row = gate[n:n + 1, :]
                    beats = jnp.logical_and(
                        jnp.logical_or(gate > row, jnp.logical_and(gate == row, sub < n)), sub < nb)
                    rank = jnp.sum(beats.astype(F32), axis=0, keepdims=True)
                    keep = jnp.logical_or(jnp.logical_and(rank < MOBA_TOPK, n < own), n == own)
                    penalty = jnp.where(sub == n, jnp.where(keep, 0.0, NEG_INF), penalty)
                pen_t = jnp.transpose(
                    jnp.concatenate([penalty, jnp.zeros((LANES - GATE_ROWS, blk), F32)], axis=0))
                qs_ref[h - g0] = jnp.concatenate([q, pen_t.astype(BF16)], axis=1)

        for idx in range(len(maps)):
            mx_ref[idx] = jnp.full((blk, LANES), -jnp.inf, F32)

        def logits_pass(n, carry, g0=g0, maps=maps):
            start = pl.multiple_of(n * blk, blk)
            tile_dist = jnp.minimum(own - n, 2)
            if mode == "moba":
                onehot = jnp.where(lane == n, 1.0, 0.0).astype(BF16)
            for idx, (h, mi) in enumerate(maps):
                hs = slice(h * LANES, (h + 1) * LANES)
                kn = k_ref[pl.ds(start, blk), hs]
                if mode == "moba":
                    kn = jnp.concatenate([kn, onehot], axis=1)
                s = _qk(qs_ref[idx], kn) * (scale * LOG2E) + bias_ref[h, tile_dist]
                s_ref[idx, n] = s
                part = s[:, :LANES]
                for r in range(1, reps):
                    part = jnp.maximum(part, s[:, r * LANES:(r + 1) * LANES])
                mx_ref[idx] = jnp.maximum(mx_ref[idx], part)
            return carry

        lax.fori_loop(0, own + 1, logits_pass, 0)

        for idx in range(len(maps)):
            mb_ref[idx] = jnp.broadcast_to(jnp.max(mx_ref[idx], axis=-1, keepdims=True), (blk, LANES))
            ls_ref[idx] = jnp.zeros((blk, LANES), F32)
            acc_ref[idx] = jnp.zeros((blk, LANES), F32)

        def value_pass(n, carry, maps=maps):
            start = pl.multiple_of(n * blk, blk)
            for idx, (h, mi) in enumerate(maps):
                hs = slice(h * LANES, (h + 1) * LANES)
                s = s_ref[idx, n]
                mb = mb_ref[idx]
                ps = [jnp.exp2(s[:, r * LANES:(r + 1) * LANES] - mb) for r in range(reps)]
                tot = ps[0]
                for r in range(1, reps):
                    tot = tot + ps[r]
                ls_ref[idx] += tot
                p = jnp.concatenate(ps, axis=1).astype(BF16)
                acc_ref[idx] += jnp.dot(p, v_ref[pl.ds(start, blk), hs], preferred_element_type=F32)
            return carry

        lax.fori_loop(0, own + 1, value_pass, 0)

        for h in range(g0, g0 + group):
            hs = slice(h * LANES, (h + 1) * LANES)
            i0 = (h - g0) * n_maps
            o = acc_ref[i0] / jnp.sum(ls_ref[i0], axis=-1, keepdims=True)
            if mode == "diff":
                o2 = acc_ref[i0 + 1] / jnp.sum(ls_ref[i0 + 1], axis=-1, keepdims=True)
                o = o - p0_ref[...] * o2
                ms = jnp.mean(o * o, axis=-1, keepdims=True)
                o_ref[:, hs] = (o * lax.rsqrt(ms + EPS) * p1_ref[...]).astype(o_ref.dtype)
            else:
                out_ref[:, hs] = o

    if mode == "moba":
        o = out_ref[...]
        ms = jnp.mean(o * o, axis=-1, keepdims=True)
        o_ref[...] = (o * lax.rsqrt(ms + EPS) * p1_ref[...]).astype(o_ref.dtype)


def attention(proj, bsz, seq, col_q, col_k, col_v, bias, p0, p1, mode, scale, heads):
    blk = ATT_BLOCK
    nq = seq // blk
    width = heads * LANES
    assert width == GROUP_WIDTH and seq % blk == 0
    nb = seq // blk
    assert nb <= GATE_ROWS
    n_maps = 2 if mode == "diff" else 1
    group = ATT_MAPS_PER_PASS // n_maps
    const2 = lambda b, i: (0, 0)
    return pl.pallas_call(
        functools.partial(_attn_kernel, mode=mode, scale=scale, heads=heads, group=group),
        grid=(bsz, nq),
        in_specs=[
            pl.BlockSpec((blk, width), lambda b, i: (b * nq + i, col_q)),
            pl.BlockSpec((seq, width), lambda b, i: (b, col_k)),
            pl.BlockSpec((seq, width), lambda b, i: (b, col_v)),
            pl.BlockSpec((heads, 3, blk, blk), lambda b, i: (0, 0, 0, 0)),
            pl.BlockSpec(p0.shape, const2),
            pl.BlockSpec(p1.shape, const2),
        ],
        out_specs=pl.BlockSpec((blk, width), lambda b, i: (b * nq + i, 0)),
        out_shape=jax.ShapeDtypeStruct((bsz * seq, width), BF16),
        scratch_shapes=[
            pltpu.VMEM((ATT_MAPS_PER_PASS, blk, LANES * (1 if mode == "diff" else 2)), BF16),
            pltpu.VMEM((ATT_MAPS_PER_PASS, nb, blk, blk), F32),
            pltpu.VMEM((ATT_MAPS_PER_PASS, blk, LANES), F32),
            pltpu.VMEM((ATT_MAPS_PER_PASS, blk, LANES), F32),
            pltpu.VMEM((ATT_MAPS_PER_PASS, blk, LANES), F32),
            pltpu.VMEM((ATT_MAPS_PER_PASS, blk, LANES), F32),
            pltpu.VMEM((heads, GATE_ROWS, LANES), F32),
            pltpu.VMEM((blk, width), F32),
        ],
        compiler_params=_cparams(("parallel", "arbitrary")),
        name="attn_" + mode,
    )(proj, proj, proj, bias, p0, p1)


def _softplus(x):
    return jnp.maximum(x, 0.0) + jnp.log(1.0 + jnp.exp(-jnp.abs(x)))


def _silu(x):
    return x * jax.nn.sigmoid(x)


def _ssd_kernel(z_ref, xbc_ref, dt_ref, dtt_ref, cw_ref, cb_ref, dtb_ref, dtbt_ref, a_ref, at_ref, dsk_ref,
                nw_ref, o_ref, ext_ref, st_ref, y_ref, *, lc):
    halo = 8
    c = pl.program_id(1)

    @pl.when(c == 0)
    def _():
        ext_ref[0:halo, :] = jnp.zeros((halo, SSD_XBC), F32)
        st_ref[...] = jnp.zeros_like(st_ref)

    ext_ref[halo:halo + lc, :] = xbc_ref[...].astype(F32)
    conv = cb_ref[...] + cw_ref[SSD_CONV - 1:SSD_CONV, :] * ext_ref[halo:halo + lc, :]
    for kk in range(1, SSD_CONV):
        conv = conv + cw_ref[SSD_CONV - 1 - kk:SSD_CONV - kk, :] * ext_ref[halo - kk:halo - kk + lc, :]
    ext_ref[0:halo, :] = ext_ref[lc:lc + halo, :]
    xbc = _silu(conv)
    xs = xbc[:, :GROUP_WIDTH]

    dt_c = _softplus(dt_ref[...] + dtb_ref[...])
    dt_r = _softplus(dtt_ref[...] + dtbt_ref[...])
    a_c = dt_c * a_ref[...]
    a_r = dt_r * at_ref[...]
    row = lax.broadcasted_iota(jnp.int32, (lc, lc), 0)
    colm = lax.broadcasted_iota(jnp.int32, (lc, lc), 1)
    causal = row >= colm
    tri = causal.astype(BF16)
    tri_t = (colm >= row).astype(BF16)
    acum_c = _dot_exact_rhs(tri, a_c)
    acum_r = _dot_exact_lhs(a_r, tri_t)
    alast_c = acum_c[lc - 1:lc, :]

    lane = lax.broadcasted_iota(jnp.int32, (lc, LANES), 1)
    lo_half = lane < SSD_HEAD_DIM
    heads_per_group = SSD_HEADS // SSD_GROUPS
    for g in range(SSD_GROUPS):
        bg = xbc[:, GROUP_WIDTH + g * SSD_STATE:GROUP_WIDTH + (g + 1) * SSD_STATE].astype(BF16)
        cg = xbc[:, GROUP_WIDTH + SSD_BC + g * SSD_STATE:GROUP_WIDTH + SSD_BC + (g + 1) * SSD_STATE].astype(BF16)
        cb = _qk(cg, bg)
        for pr in range(heads_per_group // 2):
            ha = g * heads_per_group + 2 * pr
            hb = ha + 1
            tile = slice(ha * SSD_HEAD_DIM, (ha + 2) * SSD_HEAD_DIM)
            xp = xs[:, tile]

            def per_head(col_a, col_b):
                return jnp.where(lo_half, col_a, col_b)

            xdt = xp * per_head(dt_c[:, ha:ha + 1], dt_c[:, hb:hb + 1])
            ydiag = None
            for hh, keep in ((ha, lo_half), (hb, jnp.logical_not(lo_half))):
                seg = acum_c[:, hh:hh + 1] - acum_r[hh:hh + 1, :]
                decay = jnp.where(causal, jnp.exp(jnp.where(causal, seg, 0.0)), 0.0)
                mm = (cb * decay).astype(BF16)
                part = jnp.dot(mm, jnp.where(keep, xdt, 0.0).astype(BF16), preferred_element_type=F32)
                ydiag = part if ydiag is None else ydiag + part
            st = st_ref[ha // 2]
            yoff = _qk(cg, st.astype(BF16)) * per_head(jnp.exp(acum_c[:, ha:ha + 1]), jnp.exp(acum_c[:, hb:hb + 1]))
            y_ref[:, tile] = ydiag + yoff + dsk_ref[:, tile] * xp
            to_end = per_head(jnp.exp(alast_c[:, ha:ha + 1] - acum_c[:, ha:ha + 1]),
                              jnp.exp(alast_c[:, hb:hb + 1] - acum_c[:, hb:hb + 1]))
            xdec_t = jnp.transpose(xdt * to_end).astype(BF16)
            sub = lax.broadcasted_iota(jnp.int32, (LANES, SSD_STATE), 0)
            chunk_decay = jnp.where(sub < SSD_HEAD_DIM, jnp.exp(alast_c[:, ha:ha + 1]),
                                    jnp.exp(alast_c[:, hb:hb + 1]))
            st_ref[ha // 2] = st * chunk_decay + jnp.dot(xdec_t, bg, preferred_element_type=F32)

    y = y_ref[...] * _silu(z_ref[...].astype(F32))
    gw = GROUP_WIDTH // SSD_GROUPS
    for g in range(SSD_GROUPS):
        yg = y[:, g * gw:(g + 1) * gw]
        ms = jnp.mean(yg * yg, axis=-1, keepdims=True)
        o_ref[:, g * gw:(g + 1) * gw] = (yg * lax.rsqrt(ms + EPS) * nw_ref[:, g * gw:(g + 1) * gw]).astype(o_ref.dtype)


def _pad_lanes(v):
    return jnp.pad(v.astype(F32), (0, LANES - v.shape[0])).reshape(1, LANES)


def ssd_mixer(proj, dt_raw, bsz, seq, conv_w, conv_b, dt_bias, a_log, d_skip, norm_w, lc=128):
    nc = seq // lc
    t = bsz * seq
    a = -jnp.exp(a_log.astype(F32))
    dt_t = jnp.transpose(dt_raw[:, :SSD_HEADS])
    col16 = lambda v: jnp.broadcast_to(v.astype(F32)[:, None], (SSD_HEADS, LANES))
    dskip = jnp.repeat(d_skip.astype(F32), SSD_HEAD_DIM).reshape(1, GROUP_WIDTH)
    const2 = lambda b, c: (0, 0)
    return pl.pallas_call(
        functools.partial(_ssd_kernel, lc=lc),
        grid=(bsz, nc),
        in_specs=[
            pl.BlockSpec((lc, GROUP_WIDTH), lambda b, c: (b * nc + c, COL_Z)),
            pl.BlockSpec((lc, SSD_XBC), lambda b, c: (b * nc + c, COL_XBC)),
            pl.BlockSpec((lc, LANES), lambda b, c: (b * nc + c, 0)),
            pl.BlockSpec((SSD_HEADS, lc), lambda b, c: (0, b * nc + c)),
            pl.BlockSpec((SSD_CONV, SSD_XBC), const2),
            pl.BlockSpec((1, SSD_XBC), const2),
            pl.BlockSpec((1, LANES), const2),
            pl.BlockSpec((SSD_HEADS, LANES), const2),
            pl.BlockSpec((1, LANES), const2),
            pl.BlockSpec((SSD_HEADS, LANES), const2),
            pl.BlockSpec((1, GROUP_WIDTH), const2),
            pl.BlockSpec((1, GROUP_WIDTH), const2),
        ],
        out_specs=pl.BlockSpec((lc, GROUP_WIDTH), lambda b, c: (b * nc + c, 0)),
        out_shape=jax.ShapeDtypeStruct((t, GROUP_WIDTH), BF16),
        scratch_shapes=[
            pltpu.VMEM((lc + 8, SSD_XBC), F32),
            pltpu.VMEM((SSD_HEADS // 2, 2 * SSD_HEAD_DIM, SSD_STATE), F32),
            pltpu.VMEM((lc, GROUP_WIDTH), F32),
        ],
        compiler_params=_cparams(("parallel", "arbitrary")),
        name="ssd_mixer",
    )(proj, proj, dt_raw, dt_t, conv_w.reshape(SSD_CONV, SSD_XBC).astype(F32),
      conv_b.reshape(1, SSD_XBC).astype(F32), _pad_lanes(dt_bias), col16(dt_bias), _pad_lanes(a), col16(a),
      dskip, norm_w.reshape(1, GROUP_WIDTH).astype(F32))


def kernel(x, rel_bias_table, attn_norm_w, w_in, s5_lam_re, s5_lam_im, s5_log_dt, s5_b_re, s5_b_im, s5_c_re, s5_c_im, s5_d, s5_w_glu, s5_out_norm_w, diff_lam_q1, diff_lam_k1, diff_lam_q2, diff_lam_k2, diff_subln_w, moba_out_norm_w, ssd_conv_w, ssd_conv_b, ssd_dt_bias, ssd_a_log, ssd_d, ssd_norm_w, w_out, mlp_norm_w, w_up, w_down, final_norm_w):
    bsz, seq, d_model = x.shape
    depth = w_in.shape[0]
    t = bsz * seq
    x = x.reshape(t, d_model).astype(F32)

    w_in_t = jnp.swapaxes(w_in, 1, 2)
    w_dt_b = jnp.pad(w_in[:, :, PROJ_MAIN:], ((0, 0), (0, 0), (0, LANES - SSD_HEADS))).astype(BF16)

    tbl = rel_bias_table.astype(F32).T
    bias_diff = _bias_tiles(tbl[:DIFF_HEADS], ATT_BLOCK)
    bias_moba = _bias_tiles(tbl[DIFF_HEADS:], ATT_BLOCK)

    def row_scale(ssq):
        return lax.rsqrt(jnp.sum(ssq, axis=0) * (1.0 / d_model) + EPS)

    h, scale = rmsnorm(x, attn_norm_w[0], BF16), None
    for l in range(depth):
        proj = matmul_wstat([h], w_in_t, l, PROJ_MAIN, BF16, row_scale=scale, w_transposed=True)
        dt_raw = matmul(h, w_dt_b, l, LANES, F32, row_scale=scale)

        y_s5 = s5_mixer(proj, bsz, seq, s5_lam_re[l], s5_lam_im[l], s5_log_dt[l], s5_b_re[l], s5_b_im[l],
                        s5_c_re[l], s5_c_im[l], s5_d[l], s5_w_glu[l], s5_out_norm_w[l])

        lam_init = 0.8 - 0.6 * math.exp(-0.3 * l)
        lam = (jnp.exp(jnp.sum(diff_lam_q1[l].astype(F32) * diff_lam_k1[l].astype(F32)))
               - jnp.exp(jnp.sum(diff_lam_q2[l].astype(F32) * diff_lam_k2[l].astype(F32))) + lam_init)
        y_diff = attention(proj, bsz, seq, COL_DQ, COL_DK, COL_DV, bias_diff,
                           jnp.broadcast_to(lam, (1, LANES)).astype(F32),
                           (diff_subln_w[l].astype(F32) * (1.0 - lam_init)).reshape(1, LANES),
                           "diff", DIFF_HEAD_DIM ** -0.5, DIFF_HEADS)
        y_moba = attention(proj, bsz, seq, COL_MQ, COL_MK, COL_MV, bias_moba,
                           jnp.zeros((1, LANES), F32),
                           moba_out_norm_w[l].astype(F32).reshape(1, GROUP_WIDTH),
                           "moba", (GROUP_WIDTH // MOBA_HEADS) ** -0.5, MOBA_HEADS)
        y_ssd = ssd_mixer(proj, dt_raw, bsz, seq, ssd_conv_w[l], ssd_conv_b[l], ssd_dt_bias[l], ssd_a_log[l],
                          ssd_d[l], ssd_norm_w[l])

        x, h, ssq = matmul_wstat([y_s5, y_diff, y_moba, y_ssd], w_out, l, d_model, F32, residual=x,
                                 norm_w=mlp_norm_w[l])
        u, w_down_b = matmul_wstat([h], w_up, l, w_up.shape[2], BF16, act="relu2", row_scale=row_scale(ssq),
                                   cast_through=w_down)
        if l + 1 < depth:
            x, h, ssq = matmul(u, w_down_b[None], 0, d_model, F32, residual=x, norm_w=attn_norm_w[l + 1])
            scale = row_scale(ssq)
        else:
            x = matmul(u, w_down_b[None], 0, d_model, F32, residual=x)

    return rmsnorm(x, final_norm_w, F32).reshape(bsz, seq, d_model)
```

```python
import functools
import math

import jax
import jax.numpy as jnp
from jax import lax
from jax.experimental import pallas as pl
from jax.experimental.pallas import tpu as pltpu

F32 = jnp.float32
BF16 = jnp.bfloat16
EPS = 1e-6
NEG_INF = -1e30

GROUP_WIDTH = 1024
S5_CH_PER_GROUP = 16
S5_STATE = 64
DIFF_HEADS = 8
DIFF_HEAD_DIM = 64
MOBA_HEADS = 8
MOBA_BLOCK = 256
MOBA_TOPK = 3
SSD_HEAD_DIM = 64
SSD_HEADS = 16
SSD_GROUPS = 4
SSD_STATE = 128
SSD_CONV = 4
SSD_HALO = 8
SSD_BC = SSD_GROUPS * SSD_STATE
SSD_XBC = GROUP_WIDTH + 2 * SSD_BC
REL_BUCKETS = 32
REL_MAX_DIST = 128

COL_S5 = 0
COL_DQ, COL_DK, COL_DV = 1, 2, 3
COL_MQ, COL_MK, COL_MV = 4, 5, 6
COL_Z = 7
COL_XBC = 4
PROJ_MAIN = 10 * GROUP_WIDTH

LANES = 128
SUBLANES = 8
ATT_BLOCK = 256
ATT_MAPS_PER_PASS = 8
GATE_ROWS = 16
LOG2E = math.log2(math.e)
S5_LANE_CHUNK = 512
S5_CHUNKS = (GROUP_WIDTH // S5_CH_PER_GROUP) * S5_STATE // S5_LANE_CHUNK
WSTAT_PIECES = 4
VMEM_LIMIT = 56 * 1024 * 1024


def _cparams(sem):
    return pltpu.CompilerParams(dimension_semantics=sem, vmem_limit_bytes=VMEM_LIMIT)


def _rmsnorm_kernel(x_ref, w_ref, o_ref):
    x = x_ref[...].astype(F32)
    ms = jnp.mean(x * x, axis=-1, keepdims=True)
    o_ref[...] = (x * lax.rsqrt(ms + EPS) * w_ref[...]).astype(o_ref.dtype)


def rmsnorm(x, w, out_dtype, tm=256):
    t, d = x.shape
    return pl.pallas_call(
        _rmsnorm_kernel,
        grid=(t // tm,),
        in_specs=[pl.BlockSpec((tm, d), lambda i: (i, 0)), pl.BlockSpec((1, d), lambda i: (0, 0))],
        out_specs=pl.BlockSpec((tm, d), lambda i: (i, 0)),
        out_shape=jax.ShapeDtypeStruct((t, d), out_dtype),
        compiler_params=_cparams(("parallel",)),
        name="rmsnorm",
    )(x, w.reshape(1, d).astype(F32))


def _split_refs(refs, n_lhs, n_w, has_res, has_scale, has_norm, has_cast):
    it = iter(refs)
    take = lambda n: [next(it) for _ in range(n)]
    r = {"lhs": take(n_lhs), "w": take(n_w)}
    r["res"] = next(it) if has_res else None
    r["scale"] = next(it) if has_scale else None
    r["normw"] = next(it) if has_norm else None
    r["cast_in"] = next(it) if has_cast else None
    r["out"] = next(it)
    r["xw"], r["ssq"] = (next(it), next(it)) if has_norm else (None, None)
    r["cast_out"] = next(it) if has_cast else None
    r["scratch"] = list(it)
    return r


def _mm_epilogue(acc, r, act):
    reps = acc.shape[1] // LANES
    if r["scale"] is not None:
        acc = acc * jnp.concatenate([r["scale"][...]] * reps, axis=1)
    if act == "relu2":
        p = jnp.maximum(acc, 0.0)
        acc = p * p
    if r["res"] is not None:
        acc = r["res"][...] + acc
    r["out"][...] = acc.astype(r["out"].dtype)
    if r["normw"] is not None:
        r["xw"][...] = (acc * r["normw"][...]).astype(BF16)
        sq = acc * acc
        part = sq[:, :LANES]
        for c in range(1, reps):
            part = part + sq[:, c * LANES:(c + 1) * LANES]
        r["ssq"][...] = part


def _mm_kernel(*refs, nk, act, flags):
    r = _split_refs(refs, 1, 1, *flags)
    a_ref, w_ref, acc_ref = r["lhs"][0], r["w"][0], r["scratch"][0]
    k = pl.program_id(2)

    def partial_dot():
        return jnp.dot(a_ref[...], w_ref[...], preferred_element_type=F32)

    if nk == 1:
        _mm_epilogue(partial_dot(), r, act)
        return

    @pl.when(k == 0)
    def _():
        acc_ref[...] = partial_dot()

    @pl.when(jnp.logical_and(k > 0, k < nk - 1))
    def _():
        acc_ref[...] += partial_dot()

    @pl.when(k == nk - 1)
    def _():
        _mm_epilogue(acc_ref[...] + partial_dot(), r, act)


def _norm_out(m, n_out, tm, tn, nj, idx, idx3):
    specs = [pl.BlockSpec((tm, tn), idx), pl.BlockSpec((None, tm, LANES), idx3)]
    shapes = [jax.ShapeDtypeStruct((m, n_out), BF16), jax.ShapeDtypeStruct((nj, m, LANES), F32)]
    return specs, shapes


def matmul(lhs, w3, layer, n_out, out_dtype, act=None, residual=None, row_scale=None, norm_w=None,
           tm=1024, tn=1024, tk=2048):
    m, kdim = lhs.shape[0], w3.shape[1]
    tm, tn, tk = min(tm, m), min(tn, n_out), min(tk, kdim)
    nk = kdim // tk
    assert m % tm == 0 and n_out % tn == 0 and kdim % tk == 0
    tile = lambda i, j, k: (i, j)
    in_specs = [pl.BlockSpec((tm, tk), lambda i, j, k: (i, k)),
                pl.BlockSpec((None, tk, tn), lambda i, j, k: (layer, k, j))]
    args = [lhs, w3]
    if residual is not None:
        in_specs.append(pl.BlockSpec((tm, tn), tile))
        args.append(residual)
    if row_scale is not None:
        in_specs.append(pl.BlockSpec((tm, LANES), lambda i, j, k: (i, 0)))
        args.append(row_scale)
    out_specs = [pl.BlockSpec((tm, tn), tile)]
    out_shape = [jax.ShapeDtypeStruct((m, n_out), out_dtype)]
    if norm_w is not None:
        in_specs.append(pl.BlockSpec((1, tn), lambda i, j, k: (0, j)))
        args.append(norm_w.reshape(1, n_out).astype(F32))
        specs, shapes = _norm_out(m, n_out, tm, tn, n_out // tn, tile, lambda i, j, k: (j, i, 0))
        out_specs += specs
        out_shape += shapes
    flags = (residual is not None, row_scale is not None, norm_w is not None, False)
    res = pl.pallas_call(
        functools.partial(_mm_kernel, nk=nk, act=act, flags=flags),
        grid=(m // tm, n_out // tn, nk),
        in_specs=in_specs,
        out_specs=out_specs,
        out_shape=out_shape,
        scratch_shapes=[pltpu.VMEM((tm, tn) if nk > 1 else (8, LANES), F32)],
        compiler_params=_cparams(("parallel", "parallel", "arbitrary")),
        name="matmul_" + (act or "lin") + ("_res" if residual is not None else ""),
    )(*args)
    return res if norm_w is not None else res[0]


def _mm_wstat_kernel(*refs, act, n_lhs, flags, w_transposed):
    r = _split_refs(refs, n_lhs, WSTAT_PIECES, *flags)
    wb_ref = r["scratch"][0]
    kp = r["w"][0].shape[1 if w_transposed else 0]

    @pl.when(pl.program_id(1) == 0)
    def _():
        for p, w_ref in enumerate(r["w"]):
            if w_transposed:
                wb_ref[:, p * kp:(p + 1) * kp] = w_ref[...].astype(BF16)
            else:
                wb_ref[p * kp:(p + 1) * kp, :] = w_ref[...].astype(BF16)

    if r["cast_in"] is not None:
        r["cast_out"][...] = r["cast_in"][...].astype(BF16)

    kw = r["lhs"][0].shape[1]
    acc = None
    for i, a_ref in enumerate(r["lhs"]):
        if w_transposed:
            p = lax.dot_general(a_ref[...], wb_ref[:, i * kw:(i + 1) * kw], (((1,), (1,)), ((), ())),
                                preferred_element_type=F32)
        else:
            p = jnp.dot(a_ref[...], wb_ref[i * kw:(i + 1) * kw, :], preferred_element_type=F32)
        acc = p if acc is None else acc + p
    _mm_epilogue(acc, r, act)


def matmul_wstat(lhs, w3, layer, n_out, out_dtype, act=None, residual=None, row_scale=None, norm_w=None,
                 w_transposed=False, cast_through=None, tm=1024, tn=512):
    lhs = list(lhs)
    m = lhs[0].shape[0]
    kdim = w3.shape[2 if w_transposed else 1]
    assert sum(a.shape[1] for a in lhs) == kdim and all(a.shape[1] == lhs[0].shape[1] for a in lhs)
    assert m % tm == 0 and n_out % tn == 0 and kdim % WSTAT_PIECES == 0
    ni, nj, kp = m // tm, n_out // tn, kdim // WSTAT_PIECES
    assert ni > WSTAT_PIECES
    tile = lambda j, i: (i, j)
    in_specs = [pl.BlockSpec((tm, a.shape[1]), lambda j, i: (i, 0)) for a in lhs]
    for p in range(WSTAT_PIECES):
        col = lambda j, i, p=p: jnp.minimum(j + (i > p).astype(jnp.int32), nj - 1)
        if w_transposed:
            in_specs.append(pl.BlockSpec((None, tn, kp), lambda j, i, p=p, col=col: (layer, col(j, i), p)))
        else:
            in_specs.append(pl.BlockSpec((None, kp, tn), lambda j, i, p=p, col=col: (layer, p, col(j, i))))
    args = lhs + [w3] * WSTAT_PIECES
    if residual is not None:
        in_specs.append(pl.BlockSpec((tm, tn), tile))
        args.append(residual)
    if row_scale is not None:
        in_specs.append(pl.BlockSpec((tm, LANES), lambda j, i: (i, 0)))
        args.append(row_scale)
    if norm_w is not None:
        in_specs.append(pl.BlockSpec((1, tn), lambda j, i: (0, j)))
        args.append(norm_w.reshape(1, n_out).astype(F32))
    out_specs = [pl.BlockSpec((tm, tn), tile)]
    out_shape = [jax.ShapeDtypeStruct((m, n_out), out_dtype)]
    if norm_w is not None:
        specs, shapes = _norm_out(m, n_out, tm, tn, nj, tile, lambda j, i: (j, i, 0))
        out_specs += specs
        out_shape += shapes
    if cast_through is not None:
        steps = nj * ni
        rows, cols = cast_through.shape[1:]
        assert rows % steps == 0
        slab = rows // steps
        in_specs.append(pl.BlockSpec((None, slab, cols), lambda j, i: (layer, j * ni + i, 0)))
        args.append(cast_through)
        out_specs.append(pl.BlockSpec((slab, cols), lambda j, i: (j * ni + i, 0)))
        out_shape.append(jax.ShapeDtypeStruct((rows, cols), BF16))
    flags = (residual is not None, row_scale is not None, norm_w is not None, cast_through is not None)
    res = pl.pallas_call(
        functools.partial(_mm_wstat_kernel, act=act, n_lhs=len(lhs), flags=flags, w_transposed=w_transposed),
        grid=(nj, ni),
        in_specs=in_specs,
        out_specs=out_specs,
        out_shape=out_shape,
        scratch_shapes=[pltpu.VMEM((tn, kdim) if w_transposed else (kdim, tn), BF16)],
        compiler_params=_cparams(("arbitrary", "arbitrary")),
        name="matmul_wstat_" + (act or "lin") + ("_res" if residual is not None else ""),
    )(*args)
    return res if len(res) > 1 else res[0]


def _split3(x):
    hi = x.astype(BF16)
    r1 = x - hi.astype(F32)
    mid = r1.astype(BF16)
    lo = (r1 - mid.astype(F32)).astype(BF16)
    return hi, mid, lo


def _dot_exact_lhs(x, sel):
    hi, mid, lo = _split3(x)
    d = functools.partial(jnp.dot, preferred_element_type=F32)
    return d(hi, sel) + d(mid, sel) + d(lo, sel)


def _dot_exact_rhs(sel, x):
    hi, mid, lo = _split3(x)
    d = functools.partial(jnp.dot, preferred_element_type=F32)
    return d(sel, hi) + d(sel, mid) + d(sel, lo)


def _gelu_tanh(x):
    c = math.sqrt(2.0 / math.pi)
    return 0.5 * x * (1.0 + jnp.tanh(c * (x + 0.044715 * (x * x * x))))


def _s5_kernel(u_ref, perm_ref, permt_ref, bre_ref, bim_ref, cre_ref, cim_ref, are_ref, aim_ref,
               apr_ref, api_ref, d_ref, wglu_ref, nw_ref, o_ref, sr_ref, si_ref, y_ref, st_re, st_im, *, tc):
    nsub = SUBLANES
    m = tc // nsub

    @pl.when(pl.program_id(1) == 0)
    def _():
        st_re[...] = jnp.zeros_like(st_re)
        st_im[...] = jnp.zeros_like(st_im)

    up = jnp.dot(perm_ref[...], u_ref[...], preferred_element_type=F32).astype(BF16)

    cw = S5_LANE_CHUNK
    uw = cw // S5_STATE * S5_CH_PER_GROUP
    sub = lax.broadcasted_iota(jnp.int32, (nsub, cw), 0)
    for j in range(S5_CHUNKS):
        cols = slice(j * cw, (j + 1) * cw)
        uj = up[:, j * uw:(j + 1) * uw]
        sr_ref[...] = jnp.dot(uj, bre_ref[j], preferred_element_type=F32)
        si_ref[...] = jnp.dot(uj, bim_ref[j], preferred_element_type=F32)
        ar = jnp.broadcast_to(are_ref[:, cols], (nsub, cw))
        ai = jnp.broadcast_to(aim_ref[:, cols], (nsub, cw))

        def step(t, carry, ar=ar, ai=ai):
            pr, pi = carry
            rows = pl.ds(pl.multiple_of(t * nsub, nsub), nsub)
            nr = ar * pr - ai * pi + sr_ref[rows, :]
            ni = ar * pi + ai * pr + si_ref[rows, :]
            sr_ref[rows, :] = nr
            si_ref[rows, :] = ni
            return nr, ni

        zero = jnp.zeros((nsub, cw), F32)
        loc_r, loc_i = lax.fori_loop(0, m, step, (zero, zero), unroll=4)

        er, ei = st_re[:, cols], st_im[:, cols]
        amr, ami = apr_ref[m - 1:m, cols], api_ref[m - 1:m, cols]
        ent_r, ent_i = zero, zero
        for q in range(nsub):
            ent_r = jnp.where(sub == q, er, ent_r)
            ent_i = jnp.where(sub == q, ei, ent_i)
            er, ei = (amr * er - ami * ei + loc_r[q:q + 1, :], amr * ei + ami * er + loc_i[q:q + 1, :])
        st_re[:, cols] = er
        st_im[:, cols] = ei

        def fix(t, carry, ent_r=ent_r, ent_i=ent_i, cols=cols):
            rows = pl.ds(pl.multiple_of(t * nsub, nsub), nsub)
            pr = apr_ref[pl.ds(t, 1), cols]
            pi = api_ref[pl.ds(t, 1), cols]
            sr_ref[rows, :] = sr_ref[rows, :] + (pr * ent_r - pi * ent_i)
            si_ref[rows, :] = si_ref[rows, :] + (pr * ent_i + pi * ent_r)
            return carry

        lax.fori_loop(0, m, fix, 0, unroll=4)
        y_ref[:, j * uw:(j + 1) * uw] = (
            jnp.dot(sr_ref[...].astype(BF16), cre_ref[j], preferred_element_type=F32)
            - jnp.dot(si_ref[...].astype(BF16), cim_ref[j], preferred_element_type=F32))

    y = _gelu_tanh(y_ref[...] + d_ref[...] * up.astype(F32))
    gate = jnp.dot(y.astype(BF16), wglu_ref[...], preferred_element_type=F32)
    out = y * jax.nn.sigmoid(gate)
    ms = jnp.mean(out * out, axis=-1, keepdims=True)
    outp = (out * lax.rsqrt(ms + EPS) * nw_ref[...]).astype(BF16)
    o_ref[...] = jnp.dot(permt_ref[...], outp, preferred_element_type=F32).astype(o_ref.dtype)


def _s5_discretise(lam_re, lam_im, log_dt, b_re, b_im):
    dt = jnp.exp(log_dt.astype(F32))[:, None]
    lr, li = lam_re.astype(F32), lam_im.astype(F32)
    mag = jnp.exp(lr * dt)
    ab_re = mag * jnp.cos(li * dt)
    ab_im = mag * jnp.sin(li * dt)
    den = lr * lr + li * li
    f_re = ((ab_re - 1.0) * lr + ab_im * li) / den
    f_im = (ab_im * lr - (ab_re - 1.0) * li) / den
    br, bi = b_re.astype(F32), b_im.astype(F32)
    bb_re = f_re[..., None] * br - f_im[..., None] * bi
    bb_im = f_re[..., None] * bi + f_im[..., None] * br
    return ab_re, ab_im, bb_re, bb_im


def _block_diag(blocks, per):
    g, r, c = blocks.shape
    b = blocks.reshape(g // per, per, r, c)
    eye = jnp.eye(per, dtype=blocks.dtype)
    return jnp.einsum("nirc,ij->nirjc", b, eye).reshape(g // per, per * r, per * c)


def s5_mixer(proj, bsz, seq, lam_re, lam_im, log_dt, b_re, b_im, c_re, c_im, d_skip, w_glu, norm_w, tc=512):
    per = S5_LANE_CHUNK // S5_STATE
    ab_re, ab_im, bb_re, bb_im = _s5_discretise(lam_re, lam_im, log_dt, b_re, b_im)
    bre = _block_diag(jnp.swapaxes(bb_re, 1, 2), per).astype(BF16)
    bim = _block_diag(jnp.swapaxes(bb_im, 1, 2), per).astype(BF16)
    cre = _block_diag(jnp.swapaxes(c_re.astype(F32), 1, 2), per).astype(BF16)
    cim = _block_diag(jnp.swapaxes(c_im.astype(F32), 1, 2), per).astype(BF16)
    nstate = ab_re.size
    tc = min(tc, seq)
    nct = seq // tc
    m = tc // SUBLANES
    steps = jnp.arange(1, m + 1, dtype=F32)[:, None, None]
    dt = jnp.exp(log_dt.astype(F32))[None, :, None]
    pow_mag = jnp.exp(steps * lam_re.astype(F32)[None] * dt)
    pow_ang = steps * lam_im.astype(F32)[None] * dt
    apow_re = (pow_mag * jnp.cos(pow_ang)).reshape(m, nstate)
    apow_im = (pow_mag * jnp.sin(pow_ang)).reshape(m, nstate)
    r = jnp.arange(tc)
    perm = (r[None, :] == ((r % SUBLANES) * m + r // SUBLANES)[:, None]).astype(BF16)
    uw = per * S5_CH_PER_GROUP
    const2 = lambda b, c: (0, 0)
    const3 = lambda b, c: (0, 0, 0)
    return pl.pallas_call(
        functools.partial(_s5_kernel, tc=tc),
        grid=(bsz, nct),
        in_specs=[
            pl.BlockSpec((tc, GROUP_WIDTH), lambda b, c: (b * nct + c, COL_S5)),
            pl.BlockSpec((tc, tc), const2),
            pl.BlockSpec((tc, tc), const2),
            pl.BlockSpec((S5_CHUNKS, uw, S5_LANE_CHUNK), const3),
            pl.BlockSpec((S5_CHUNKS, uw, S5_LANE_CHUNK), const3),
            pl.BlockSpec((S5_CHUNKS, S5_LANE_CHUNK, uw), const3),
            pl.BlockSpec((S5_CHUNKS, S5_LANE_CHUNK, uw), const3),
            pl.BlockSpec((1, nstate), const2),
            pl.BlockSpec((1, nstate), const2),
            pl.BlockSpec((m, nstate), const2),
            pl.BlockSpec((m, nstate), const2),
            pl.BlockSpec((1, GROUP_WIDTH), const2),
            pl.BlockSpec((GROUP_WIDTH, GROUP_WIDTH), const2),
            pl.BlockSpec((1, GROUP_WIDTH), const2),
        ],
        out_specs=pl.BlockSpec((tc, GROUP_WIDTH), lambda b, c: (b * nct + c, 0)),
        out_shape=jax.ShapeDtypeStruct((bsz * seq, GROUP_WIDTH), BF16),
        scratch_shapes=[
            pltpu.VMEM((tc, S5_LANE_CHUNK), F32),
            pltpu.VMEM((tc, S5_LANE_CHUNK), F32),
            pltpu.VMEM((tc, GROUP_WIDTH), F32),
            pltpu.VMEM((1, nstate), F32),
            pltpu.VMEM((1, nstate), F32),
        ],
        compiler_params=_cparams(("parallel", "arbitrary")),
        name="s5_mixer",
    )(proj, perm, perm.T, bre, bim, cre, cim, ab_re.reshape(1, nstate), ab_im.reshape(1, nstate),
      apow_re, apow_im, d_skip.reshape(1, GROUP_WIDTH).astype(F32), w_glu.astype(BF16), norm_w.reshape(1, GROUP_WIDTH).astype(F32))


def _rel_bucket(dist):
    n = jnp.maximum(dist, 0)
    max_exact = REL_BUCKETS // 2
    log_ratio = jnp.log(jnp.maximum(n, 1).astype(F32) / max_exact) / math.log(REL_MAX_DIST / max_exact)
    large = max_exact + (log_ratio * (REL_BUCKETS - max_exact)).astype(jnp.int32)
    large = jnp.minimum(large, REL_BUCKETS - 1)
    return jnp.where(n < max_exact, n, large)


def _bias_tiles(tbl, blk):
    assert blk >= REL_MAX_DIST
    i = jnp.arange(blk)[:, None]
    j = jnp.arange(blk)[None, :]
    buckets = jnp.stack([_rel_bucket(i - j), _rel_bucket(blk + i - j), _rel_bucket(jnp.full((blk, blk), 2 * blk))])
    onehot = (buckets[..., None] == jnp.arange(REL_BUCKETS)).astype(F32)
    tiles = jnp.einsum("tijk,hk->htij", onehot, tbl, precision=lax.Precision.HIGHEST) * LOG2E
    visible = jnp.stack([i >= j, jnp.ones((blk, blk), bool), jnp.ones((blk, blk), bool)])
    return jnp.where(visible[None], tiles, NEG_INF).astype(F32)


def _qk(q, k):
    return lax.dot_general(q, k, (((1,), (1,)), ((), ())), preferred_element_type=F32)


def _attn_kernel(q_ref, k_ref, v_ref, bias_ref, p0_ref, p1_ref, o_ref,
                 qs_ref, s_ref, mx_ref, mb_ref, ls_ref, acc_ref, kmean_ref, out_ref, *, mode, scale, heads, group):
    blk = ATT_BLOCK
    own = pl.program_id(1)
    lane = lax.broadcasted_iota(jnp.int32, (blk, LANES), 1)
    n_maps = 2 if mode == "diff" else 1
    nb = k_ref.shape[0] // blk
    reps = blk // LANES

    if mode == "moba":
        @pl.when(own == 0)
        def _():
            for h in range(heads):
                kf = k_ref[:, h * LANES:(h + 1) * LANES].astype(F32).reshape(nb, blk, LANES)
                kmean = jnp.sum(kf, axis=1) * (1.0 / blk)
                kmean_ref[h] = jnp.concatenate([kmean, jnp.zeros((GATE_ROWS - nb, LANES), F32)], axis=0)

    for g0 in range(0, heads, group):
        maps = [(h, mi) for h in range(g0, g0 + group) for mi in range(n_maps)]
        for h in range(g0, g0 + group):
            hs = slice(h * LANES, (h + 1) * LANES)
            q = q_ref[:, hs]
            if mode == "diff":
                qf = q.astype(F32)
                qs_ref[(h - g0) * 2] = jnp.where(lane < DIFF_HEAD_DIM, qf, 0.0).astype(BF16)
                qs_ref[(h - g0) * 2 + 1] = jnp.where(lane >= DIFF_HEAD_DIM, qf, 0.0).astype(BF16)
            else:
                hi, mid, lo = _split3(kmean_ref[h])
                gate = _qk(hi, q) + _qk(mid, q) + _qk(lo, q)
                sub = lax.broadcasted_iota(jnp.int32, (GATE_ROWS, blk), 0)
                gate = jnp.where(sub < own, gate, NEG_INF)
                penalty = jnp.zeros((GATE_ROWS, blk), F32)
                for n in range(nb):
                    row = gate[n:n + 1, :]
                    beats = jnp.logical_and(
                        jnp.logical_or(gate > row, jnp.logical_and(gate == row, sub < n)), sub < nb)
                    rank = jnp.sum(beats.astype(F32), axis=0, keepdims=True)
                    keep = jnp.logical_or(jnp.logical_and(rank < MOBA_TOPK, n < own), n == own)
                    penalty = jnp.where(sub == n, jnp.where(keep, 0.0, NEG_INF), penalty)
                pen_t = jnp.transpose(
                    jnp.concatenate([penalty, jnp.zeros((LANES - GATE_ROWS, blk), F32)], axis=0))
                qs_ref[h - g0] = jnp.concatenate([q, pen_t.astype(BF16)], axis=1)

        for idx in range(len(maps)):
            mx_ref[idx] = jnp.full((blk, LANES), -jnp.inf, F32)

        def logits_pass(n, carry, g0=g0, maps=maps):
            start = pl.multiple_of(n * blk, blk)
            tile_dist = jnp.minimum(own - n, 2)
            if mode == "moba":
                onehot = jnp.where(lane == n, 1.0, 0.0).astype(BF16)
            for idx, (h, mi) in enumerate(maps):
                hs = slice(h * LANES, (h + 1) * LANES)
                kn = k_ref[pl.ds(start, blk), hs]
                if mode == "moba":
                    kn = jnp.concatenate([kn, onehot], axis=1)
                s = _qk(qs_ref[idx], kn) * (scale * LOG2E) + bias_ref[h, tile_dist]
                s_ref[idx, n] = s
                part = s[:, :LANES]
                for r in range(1, reps):
                    part = jnp.maximum(part, s[:, r * LANES:(r + 1) * LANES])
                mx_ref[idx] = jnp.maximum(mx_ref[idx], part)
            return carry

        lax.fori_loop(0, own + 1, logits_pass, 0)

        for idx in range(len(maps)):
            mb_ref[idx] = jnp.broadcast_to(jnp.max(mx_ref[idx], axis=-1, keepdims=True), (blk, LANES))
            ls_ref[idx] = jnp.zeros((blk, LANES), F32)
            acc_ref[idx] = jnp.zeros((blk, LANES), F32)

        def value_pass(n, carry, maps=maps):
            start = pl.multiple_of(n * blk, blk)
            for idx, (h, mi) in enumerate(maps):
                hs = slice(h * LANES, (h + 1) * LANES)
                s = s_ref[idx, n]
                mb = mb_ref[idx]
                ps = [jnp.exp2(s[:, r * LANES:(r + 1) * LANES] - mb) for r in range(reps)]
                tot = ps[0]
                for r in range(1, reps):
                    tot = tot + ps[r]
                ls_ref[idx] += tot
                p = jnp.concatenate(ps, axis=1).astype(BF16)
                acc_ref[idx] += jnp.dot(p, v_ref[pl.ds(start, blk), hs], preferred_element_type=F32)
            return carry

        lax.fori_loop(0, own + 1, value_pass, 0)

        for h in range(g0, g0 + group):
            hs = slice(h * LANES, (h + 1) * LANES)
            i0 = (h - g0) * n_maps
            o = acc_ref[i0] / jnp.sum(ls_ref[i0], axis=-1, keepdims=True)
            if mode == "diff":
                o2 = acc_ref[i0 + 1] / jnp.sum(ls_ref[i0 + 1], axis=-1, keepdims=True)
                o = o - p0_ref[...] * o2
                ms = jnp.mean(o * o, axis=-1, keepdims=True)
                o_ref[:, hs] = (o * lax.rsqrt(ms + EPS) * p1_ref[...]).astype(o_ref.dtype)
            else:
                out_ref[:, hs] = o

    if mode == "moba":
        o = out_ref[...]
        ms = jnp.mean(o * o, axis=-1, keepdims=True)
        o_ref[...] = (o * lax.rsqrt(ms + EPS) * p1_ref[...]).astype(o_ref.dtype)


def attention(proj, bsz, seq, col_q, col_k, col_v, bias, p0, p1, mode, scale, heads):
    blk = ATT_BLOCK
    nq = seq // blk
    width = heads * LANES
    assert width == GROUP_WIDTH and seq % blk == 0
    nb = seq // blk
    assert nb <= GATE_ROWS
    n_maps = 2 if mode == "diff" else 1
    group = ATT_MAPS_PER_PASS // n_maps
    const2 = lambda b, i: (0, 0)
    return pl.pallas_call(
        functools.partial(_attn_kernel, mode=mode, scale=scale, heads=heads, group=group),
        grid=(bsz, nq),
        in_specs=[
            pl.BlockSpec((blk, width), lambda b, i: (b * nq + i, col_q)),
            pl.BlockSpec((seq, width), lambda b, i: (b, col_k)),
            pl.BlockSpec((seq, width), lambda b, i: (b, col_v)),
            pl.BlockSpec((heads, 3, blk, blk), lambda b, i: (0, 0, 0, 0)),
            pl.BlockSpec(p0.shape, const2),
            pl.BlockSpec(p1.shape, const2),
        ],
        out_specs=pl.BlockSpec((blk, width), lambda b, i: (b * nq + i, 0)),
        out_shape=jax.ShapeDtypeStruct((bsz * seq, width), BF16),
        scratch_shapes=[
            pltpu.VMEM((ATT_MAPS_PER_PASS, blk, LANES * (1 if mode == "diff" else 2)), BF16),
            pltpu.VMEM((ATT_MAPS_PER_PASS, nb, blk, blk), F32),
            pltpu.VMEM((ATT_MAPS_PER_PASS, blk, LANES), F32),
            pltpu.VMEM((ATT_MAPS_PER_PASS, blk, LANES), F32),
            pltpu.VMEM((ATT_MAPS_PER_PASS, blk, LANES), F32),
            pltpu.VMEM((ATT_MAPS_PER_PASS, blk, LANES), F32),
            pltpu.VMEM((heads, GATE_ROWS, LANES), F32),
            pltpu.VMEM((blk, width), F32),
        ],
        compiler_params=_cparams(("parallel", "arbitrary")),
        name="attn_" + mode,
    )(proj, proj, proj, bias, p0, p1)


def _softplus(x):
    return jnp.maximum(x, 0.0) + jnp.log(1.0 + jnp.exp(-jnp.abs(x)))


def _silu(x):
    return x * jax.nn.sigmoid(x)


def _ssd_kernel(z_ref, xbc_ref, dt_ref, dtt_ref, cw_ref, cb_ref, dtb_ref, dtbt_ref, a_ref, at_ref, dsk_ref,
                nw_ref, o_ref, ext_ref, st_ref, y_ref, *, lc):
    halo = SSD_HALO
    c = pl.program_id(1)

    @pl.when(c == 0)
    def _():
        ext_ref[0:halo, :] = jnp.zeros((halo, SSD_XBC), F32)
        st_ref[...] = jnp.zeros_like(st_ref)

    ext_ref[halo:halo + lc, :] = xbc_ref[...].astype(F32)
    conv = cb_ref[...] + cw_ref[SSD_CONV - 1:SSD_CONV, :] * ext_ref[halo:halo + lc, :]
    for kk in range(1, SSD_CONV):
        conv = conv + cw_ref[SSD_CONV - 1 - kk:SSD_CONV - kk, :] * ext_ref[halo - kk:halo - kk + lc, :]
    ext_ref[0:halo, :] = ext_ref[lc:lc + halo, :]
    xbc = _silu(conv)
    xs = xbc[:, :GROUP_WIDTH]

    dt_c = _softplus(dt_ref[...] + dtb_ref[...])
    dt_r = _softplus(dtt_ref[...] + dtbt_ref[...])
    a_c = dt_c * a_ref[...]
    a_r = dt_r * at_ref[...]
    row = lax.broadcasted_iota(jnp.int32, (lc, lc), 0)
    colm = lax.broadcasted_iota(jnp.int32, (lc, lc), 1)
    causal = row >= colm
    tri = causal.astype(BF16)
    tri_t = (colm >= row).astype(BF16)
    acum_c = _dot_exact_rhs(tri, a_c)
    acum_r = _dot_exact_lhs(a_r, tri_t)
    alast_c = acum_c[lc - 1:lc, :]

    lane = lax.broadcasted_iota(jnp.int32, (lc, LANES), 1)
    lo_half = lane < SSD_HEAD_DIM
    heads_per_group = SSD_HEADS // SSD_GROUPS
    for g in range(SSD_GROUPS):
        bg = xbc[:, GROUP_WIDTH + g * SSD_STATE:GROUP_WIDTH + (g + 1) * SSD_STATE].astype(BF16)
        cg = xbc[:, GROUP_WIDTH + SSD_BC + g * SSD_STATE:GROUP_WIDTH + SSD_BC + (g + 1) * SSD_STATE].astype(BF16)
        cb = _qk(cg, bg)
        for pr in range(heads_per_group // 2):
            ha = g * heads_per_group + 2 * pr
            hb = ha + 1
            tile = slice(ha * SSD_HEAD_DIM, (ha + 2) * SSD_HEAD_DIM)
            xp = xs[:, tile]

            def per_head(col_a, col_b):
                return jnp.where(lo_half, col_a, col_b)

            xdt = xp * per_head(dt_c[:, ha:ha + 1], dt_c[:, hb:hb + 1])
            ydiag = None
            for hh, keep in ((ha, lo_half), (hb, jnp.logical_not(lo_half))):
                seg = acum_c[:, hh:hh + 1] - acum_r[hh:hh + 1, :]
                decay = jnp.exp(jnp.where(causal, seg, NEG_INF))
                mm = (cb * decay).astype(BF16)
                part = jnp.dot(mm, jnp.where(keep, xdt, 0.0).astype(BF16), preferred_element_type=F32)
                ydiag = part if ydiag is None else ydiag + part
            st = st_ref[ha // 2]
            yoff = _qk(cg, st.astype(BF16)) * per_head(jnp.exp(acum_c[:, ha:ha + 1]), jnp.exp(acum_c[:, hb:hb + 1]))
            y_ref[:, tile] = ydiag + yoff + dsk_ref[:, tile] * xp
            to_end = per_head(jnp.exp(alast_c[:, ha:ha + 1] - acum_c[:, ha:ha + 1]),
                              jnp.exp(alast_c[:, hb:hb + 1] - acum_c[:, hb:hb + 1]))
            xdec_t = jnp.transpose(xdt * to_end).astype(BF16)
            sub = lax.broadcasted_iota(jnp.int32, (LANES, SSD_STATE), 0)
            chunk_decay = jnp.where(sub < SSD_HEAD_DIM, jnp.exp(alast_c[:, ha:ha + 1]),
                                    jnp.exp(alast_c[:, hb:hb + 1]))
            st_ref[ha // 2] = st * chunk_decay + jnp.dot(xdec_t, bg, preferred_element_type=F32)

    y = y_ref[...] * _silu(z_ref[...].astype(F32))
    gw = GROUP_WIDTH // SSD_GROUPS
    for g in range(SSD_GROUPS):
        yg = y[:, g * gw:(g + 1) * gw]
        ms = jnp.mean(yg * yg, axis=-1, keepdims=True)
        o_ref[:, g * gw:(g + 1) * gw] = (yg * lax.rsqrt(ms + EPS) * nw_ref[:, g * gw:(g + 1) * gw]).astype(o_ref.dtype)


def _pad_lanes(v):
    return jnp.pad(v.astype(F32), (0, LANES - v.shape[0])).reshape(1, LANES)


def ssd_mixer(proj, dt_raw, bsz, seq, conv_w, conv_b, dt_bias, a_log, d_skip, norm_w, lc=128):
    nc = seq // lc
    t = bsz * seq
    a = -jnp.exp(a_log.astype(F32))
    dt_t = jnp.transpose(dt_raw[:, :SSD_HEADS])
    col16 = lambda v: jnp.broadcast_to(v.astype(F32)[:, None], (SSD_HEADS, LANES))
    dskip = jnp.repeat(d_skip.astype(F32), SSD_HEAD_DIM).reshape(1, GROUP_WIDTH)
    const2 = lambda b, c: (0, 0)
    return pl.pallas_call(
        functools.partial(_ssd_kernel, lc=lc),
        grid=(bsz, nc),
        in_specs=[
            pl.BlockSpec((lc, GROUP_WIDTH), lambda b, c: (b * nc + c, COL_Z)),
            pl.BlockSpec((lc, SSD_XBC), lambda b, c: (b * nc + c, COL_XBC)),
            pl.BlockSpec((lc, LANES), lambda b, c: (b * nc + c, 0)),
            pl.BlockSpec((SSD_HEADS, lc), lambda b, c: (0, b * nc + c)),
            pl.BlockSpec((SSD_CONV, SSD_XBC), const2),
            pl.BlockSpec((1, SSD_XBC), const2),
            pl.BlockSpec((1, LANES), const2),
            pl.BlockSpec((SSD_HEADS, LANES), const2),
            pl.BlockSpec((1, LANES), const2),
            pl.BlockSpec((SSD_HEADS, LANES), const2),
            pl.BlockSpec((1, GROUP_WIDTH), const2),
            pl.BlockSpec((1, GROUP_WIDTH), const2),
        ],
        out_specs=pl.BlockSpec((lc, GROUP_WIDTH), lambda b, c: (b * nc + c, 0)),
        out_shape=jax.ShapeDtypeStruct((t, GROUP_WIDTH), BF16),
        scratch_shapes=[
            pltpu.VMEM((lc + SSD_HALO, SSD_XBC), F32),
            pltpu.VMEM((SSD_HEADS // 2, 2 * SSD_HEAD_DIM, SSD_STATE), F32),
            pltpu.VMEM((lc, GROUP_WIDTH), F32),
        ],
        compiler_params=_cparams(("parallel", "arbitrary")),
        name="ssd_mixer",
    )(proj, proj, dt_raw, dt_t, conv_w.reshape(SSD_CONV, SSD_XBC).astype(F32),
      conv_b.reshape(1, SSD_XBC).astype(F32), _pad_lanes(dt_bias), col16(dt_bias), _pad_lanes(a), col16(a),
      dskip, norm_w.reshape(1, GROUP_WIDTH).astype(F32))


def kernel(x, rel_bias_table, attn_norm_w, w_in, s5_lam_re, s5_lam_im, s5_log_dt, s5_b_re, s5_b_im, s5_c_re, s5_c_im, s5_d, s5_w_glu, s5_out_norm_w, diff_lam_q1, diff_lam_k1, diff_lam_q2, diff_lam_k2, diff_subln_w, moba_out_norm_w, ssd_conv_w, ssd_conv_b, ssd_dt_bias, ssd_a_log, ssd_d, ssd_norm_w, w_out, mlp_norm_w, w_up, w_down, final_norm_w):
    bsz, seq, d_model = x.shape
    depth = w_in.shape[0]
    t = bsz * seq
    x = x.reshape(t, d_model).astype(F32)

    w_in_t = jnp.swapaxes(w_in, 1, 2)
    w_dt_b = jnp.pad(w_in[:, :, PROJ_MAIN:], ((0, 0), (0, 0), (0, LANES - SSD_HEADS))).astype(BF16)

    tbl = rel_bias_table.astype(F32).T
    bias_diff = _bias_tiles(tbl[:DIFF_HEADS], ATT_BLOCK)
    bias_moba = _bias_tiles(tbl[DIFF_HEADS:], ATT_BLOCK)

    def row_scale(ssq):
        r = lax.rsqrt(jnp.sum(ssq, axis=(0, 2)) * (1.0 / d_model) + EPS)
        return jnp.broadcast_to(r[:, None], (t, LANES))

    h, scale = rmsnorm(x, attn_norm_w[0], BF16), None
    for l in range(depth):
        proj = matmul_wstat([h], w_in_t, l, PROJ_MAIN, BF16, row_scale=scale, w_transposed=True)
        dt_raw = matmul(h, w_dt_b, l, LANES, F32, row_scale=scale)

        y_s5 = s5_mixer(proj, bsz, seq, s5_lam_re[l], s5_lam_im[l], s5_log_dt[l], s5_b_re[l], s5_b_im[l],
                        s5_c_re[l], s5_c_im[l], s5_d[l], s5_w_glu[l], s5_out_norm_w[l])

        lam_init = 0.8 - 0.6 * math.exp(-0.3 * l)
        lam = (jnp.exp(jnp.sum(diff_lam_q1[l].astype(F32) * diff_lam_k1[l].astype(F32)))
               - jnp.exp(jnp.sum(diff_lam_q2[l].astype(F32) * diff_lam_k2[l].astype(F32))) + lam_init)
        y_diff = attention(proj, bsz, seq, COL_DQ, COL_DK, COL_DV, bias_diff,
                           jnp.broadcast_to(lam, (1, LANES)).astype(F32),
                           (diff_subln_w[l].astype(F32) * (1.0 - lam_init)).reshape(1, LANES),
                           "diff", DIFF_HEAD_DIM ** -0.5, DIFF_HEADS)
        y_moba = attention(proj, bsz, seq, COL_MQ, COL_MK, COL_MV, bias_moba,
                           jnp.zeros((1, LANES), F32),
                           moba_out_norm_w[l].astype(F32).reshape(1, GROUP_WIDTH),
                           "moba", (GROUP_WIDTH // MOBA_HEADS) ** -0.5, MOBA_HEADS)
        y_ssd = ssd_mixer(proj, dt_raw, bsz, seq, ssd_conv_w[l], ssd_conv_b[l], ssd_dt_bias[l], ssd_a_log[l],
                          ssd_d[l], ssd_norm_w[l])

        x, h, ssq = matmul_wstat([y_s5, y_diff, y_moba, y_ssd], w_out, l, d_model, F32, residual=x,
                                 norm_w=mlp_norm_w[l])
        u, w_down_b = matmul_wstat([h], w_up, l, w_up.shape[2], BF16, act="relu2", row_scale=row_scale(ssq),
                                   cast_through=w_down)
        if l + 1 < depth:
            x, h, ssq = matmul(u, w_down_b[None], 0, d_model, F32, residual=x, norm_w=attn_norm_w[l + 1])
            scale = row_scale(ssq)
        else:
            x = matmul(u, w_down_b[None], 0, d_model, F32, residual=x)

    return rmsnorm(x, final_norm_w, F32).reshape(bsz, seq, d_model)
```

```python
import functools
import math

import jax
import jax.numpy as jnp
from jax import lax
from jax.experimental import pallas as pl
from jax.experimental.pallas import tpu as pltpu

F32 = jnp.float32
BF16 = jnp.bfloat16
EPS = 1e-6
NEG_INF = -1e30

GROUP_WIDTH = 1024
S5_CH_PER_GROUP = 16
S5_STATE = 64
DIFF_HEADS = 8
DIFF_HEAD_DIM = 64
MOBA_HEADS = 8
MOBA_BLOCK = 256
MOBA_TOPK = 3
SSD_HEAD_DIM = 64
SSD_HEADS = 16
SSD_GROUPS = 4
SSD_STATE = 128
SSD_CONV = 4
SSD_HALO = 8
SSD_BC = SSD_GROUPS * SSD_STATE
SSD_XBC = GROUP_WIDTH + 2 * SSD_BC
REL_BUCKETS = 32
REL_MAX_DIST = 128

COL_S5 = 0
COL_DQ, COL_DK, COL_DV = 1, 2, 3
COL_MQ, COL_MK, COL_MV = 4, 5, 6
COL_Z = 7
COL_XBC = 4
PROJ_MAIN = 10 * GROUP_WIDTH

LANES = 128
SUBLANES = 8
ATT_BLOCK = 256
ATT_MAPS_PER_PASS = 8
ATT_STEP_UNROLL = 3
GATE_ROWS = 16
LOG2E = math.log2(math.e)
S5_LANE_CHUNK = 512
S5_CHUNKS = (GROUP_WIDTH // S5_CH_PER_GROUP) * S5_STATE // S5_LANE_CHUNK
WSTAT_PIECES = 4
VMEM_LIMIT = 56 * 1024 * 1024


def _cparams(sem):
    return pltpu.CompilerParams(dimension_semantics=sem, vmem_limit_bytes=VMEM_LIMIT)


def _rmsnorm_kernel(x_ref, w_ref, o_ref):
    x = x_ref[...].astype(F32)
    ms = jnp.mean(x * x, axis=-1, keepdims=True)
    o_ref[...] = (x * lax.rsqrt(ms + EPS) * w_ref[...]).astype(o_ref.dtype)


def rmsnorm(x, w, out_dtype, tm=256):
    t, d = x.shape
    return pl.pallas_call(
        _rmsnorm_kernel,
        grid=(t // tm,),
        in_specs=[pl.BlockSpec((tm, d), lambda i: (i, 0)), pl.BlockSpec((1, d), lambda i: (0, 0))],
        out_specs=pl.BlockSpec((tm, d), lambda i: (i, 0)),
        out_shape=jax.ShapeDtypeStruct((t, d), out_dtype),
        compiler_params=_cparams(("parallel",)),
        name="rmsnorm",
    )(x, w.reshape(1, d).astype(F32))


def _split_refs(refs, n_lhs, n_w, has_res, has_scale, has_norm, has_cast):
    it = iter(refs)
    take = lambda n: [next(it) for _ in range(n)]
    r = {"lhs": take(n_lhs), "w": take(n_w)}
    r["res"] = next(it) if has_res else None
    r["scale"] = next(it) if has_scale else None
    r["normw"] = next(it) if has_norm else None
    r["cast_in"] = next(it) if has_cast else None
    r["out"] = next(it)
    r["xw"], r["ssq"] = (next(it), next(it)) if has_norm else (None, None)
    r["cast_out"] = next(it) if has_cast else None
    r["scratch"] = list(it)
    return r


def _mm_epilogue(acc, r, act):
    reps = acc.shape[1] // LANES
    if r["scale"] is not None:
        acc = acc * jnp.concatenate([r["scale"][...]] * reps, axis=1)
    if act == "relu2":
        p = jnp.maximum(acc, 0.0)
        acc = p * p
    if r["res"] is not None:
        acc = r["res"][...] + acc
    r["out"][...] = acc.astype(r["out"].dtype)
    if r["normw"] is not None:
        r["xw"][...] = (acc * r["normw"][...]).astype(BF16)
        sq = acc * acc
        part = sq[:, :LANES]
        for c in range(1, reps):
            part = part + sq[:, c * LANES:(c + 1) * LANES]
        r["ssq"][...] = part


def _mm_kernel(*refs, nk, act, flags):
    r = _split_refs(refs, 1, 1, *flags)
    a_ref, w_ref, acc_ref = r["lhs"][0], r["w"][0], r["scratch"][0]
    k = pl.program_id(2)

    def partial_dot():
        return jnp.dot(a_ref[...], w_ref[...], preferred_element_type=F32)

    if nk == 1:
        _mm_epilogue(partial_dot(), r, act)
        return

    @pl.when(k == 0)
    def _():
        acc_ref[...] = partial_dot()

    @pl.when(jnp.logical_and(k > 0, k < nk - 1))
    def _():
        acc_ref[...] += partial_dot()

    @pl.when(k == nk - 1)
    def _():
        _mm_epilogue(acc_ref[...] + partial_dot(), r, act)


def _norm_out(m, n_out, tm, tn, nj, idx, idx3):
    specs = [pl.BlockSpec((tm, tn), idx), pl.BlockSpec((None, tm, LANES), idx3)]
    shapes = [jax.ShapeDtypeStruct((m, n_out), BF16), jax.ShapeDtypeStruct((nj, m, LANES), F32)]
    return specs, shapes


def matmul(lhs, w3, layer, n_out, out_dtype, act=None, residual=None, row_scale=None, norm_w=None,
           tm=1024, tn=1024, tk=2048):
    m, kdim = lhs.shape[0], w3.shape[1]
    tm, tn, tk = min(tm, m), min(tn, n_out), min(tk, kdim)
    nk = kdim // tk
    assert m % tm == 0 and n_out % tn == 0 and kdim % tk == 0
    tile = lambda i, j, k: (i, j)
    in_specs = [pl.BlockSpec((tm, tk), lambda i, j, k: (i, k)),
                pl.BlockSpec((None, tk, tn), lambda i, j, k: (layer, k, j))]
    args = [lhs, w3]
    if residual is not None:
        in_specs.append(pl.BlockSpec((tm, tn), tile))
        args.append(residual)
    if row_scale is not None:
        in_specs.append(pl.BlockSpec((tm, LANES), lambda i, j, k: (i, 0)))
        args.append(row_scale)
    out_specs = [pl.BlockSpec((tm, tn), tile)]
    out_shape = [jax.ShapeDtypeStruct((m, n_out), out_dtype)]
    if norm_w is not None:
        in_specs.append(pl.BlockSpec((1, tn), lambda i, j, k: (0, j)))
        args.append(norm_w.reshape(1, n_out).astype(F32))
        specs, shapes = _norm_out(m, n_out, tm, tn, n_out // tn, tile, lambda i, j, k: (j, i, 0))
        out_specs += specs
        out_shape += shapes
    flags = (residual is not None, row_scale is not None, norm_w is not None, False)
    res = pl.pallas_call(
        functools.partial(_mm_kernel, nk=nk, act=act, flags=flags),
        grid=(m // tm, n_out // tn, nk),
        in_specs=in_specs,
        out_specs=out_specs,
        out_shape=out_shape,
        scratch_shapes=[pltpu.VMEM((tm, tn) if nk > 1 else (8, LANES), F32)],
        compiler_params=_cparams(("parallel", "parallel", "arbitrary")),
        name="matmul_" + (act or "lin") + ("_res" if residual is not None else ""),
    )(*args)
    return res if norm_w is not None else res[0]


def _mm_wstat_kernel(*refs, act, n_lhs, flags, w_transposed):
    r = _split_refs(refs, n_lhs, WSTAT_PIECES, *flags)
    wb_ref = r["scratch"][0]
    kp = r["w"][0].shape[1 if w_transposed else 0]

    @pl.when(pl.program_id(1) == 0)
    def _():
        for p, w_ref in enumerate(r["w"]):
            if w_transposed:
                wb_ref[:, p * kp:(p + 1) * kp] = w_ref[...].astype(BF16)
            else:
                wb_ref[p * kp:(p + 1) * kp, :] = w_ref[...].astype(BF16)

    if r["cast_in"] is not None:
        r["cast_out"][...] = r["cast_in"][...].astype(BF16)

    kw = r["lhs"][0].shape[1]
    acc = None
    for i, a_ref in enumerate(r["lhs"]):
        if w_transposed:
            p = lax.dot_general(a_ref[...], wb_ref[:, i * kw:(i + 1) * kw], (((1,), (1,)), ((), ())),
                                preferred_element_type=F32)
        else:
            p = jnp.dot(a_ref[...], wb_ref[i * kw:(i + 1) * kw, :], preferred_element_type=F32)
        acc = p if acc is None else acc + p
    _mm_epilogue(acc, r, act)


def matmul_wstat(lhs, w3, layer, n_out, out_dtype, act=None, residual=None, row_scale=None, norm_w=None,
                 w_transposed=False, cast_through=None, tm=1024, tn=512):
    lhs = list(lhs)
    m = lhs[0].shape[0]
    kdim = w3.shape[2 if w_transposed else 1]
    assert sum(a.shape[1] for a in lhs) == kdim and all(a.shape[1] == lhs[0].shape[1] for a in lhs)
    assert m % tm == 0 and n_out % tn == 0 and kdim % WSTAT_PIECES == 0
    ni, nj, kp = m // tm, n_out // tn, kdim // WSTAT_PIECES
    assert ni > WSTAT_PIECES
    tile = lambda j, i: (i, j)
    in_specs = [pl.BlockSpec((tm, a.shape[1]), lambda j, i: (i, 0)) for a in lhs]
    for p in range(WSTAT_PIECES):
        col = lambda j, i, p=p: jnp.minimum(j + (i > p).astype(jnp.int32), nj - 1)
        if w_transposed:
            in_specs.append(pl.BlockSpec((None, tn, kp), lambda j, i, p=p, col=col: (layer, col(j, i), p)))
        else:
            in_specs.append(pl.BlockSpec((None, kp, tn), lambda j, i, p=p, col=col: (layer, p, col(j, i))))
    args = lhs + [w3] * WSTAT_PIECES
    if residual is not None:
        in_specs.append(pl.BlockSpec((tm, tn), tile))
        args.append(residual)
    if row_scale is not None:
        in_specs.append(pl.BlockSpec((tm, LANES), lambda j, i: (i, 0)))
        args.append(row_scale)
    if norm_w is not None:
        in_specs.append(pl.BlockSpec((1, tn), lambda j, i: (0, j)))
        args.append(norm_w.reshape(1, n_out).astype(F32))
    out_specs = [pl.BlockSpec((tm, tn), tile)]
    out_shape = [jax.ShapeDtypeStruct((m, n_out), out_dtype)]
    if norm_w is not None:
        specs, shapes = _norm_out(m, n_out, tm, tn, nj, tile, lambda j, i: (j, i, 0))
        out_specs += specs
        out_shape += shapes
    if cast_through is not None:
        steps = nj * ni
        rows, cols = cast_through.shape[1:]
        assert rows % steps == 0
        slab = rows // steps
        in_specs.append(pl.BlockSpec((None, slab, cols), lambda j, i: (layer, j * ni + i, 0)))
        args.append(cast_through)
        out_specs.append(pl.BlockSpec((slab, cols), lambda j, i: (j * ni + i, 0)))
        out_shape.append(jax.ShapeDtypeStruct((rows, cols), BF16))
    flags = (residual is not None, row_scale is not None, norm_w is not None, cast_through is not None)
    res = pl.pallas_call(
        functools.partial(_mm_wstat_kernel, act=act, n_lhs=len(lhs), flags=flags, w_transposed=w_transposed),
        grid=(nj, ni),
        in_specs=in_specs,
        out_specs=out_specs,
        out_shape=out_shape,
        scratch_shapes=[pltpu.VMEM((tn, kdim) if w_transposed else (kdim, tn), BF16)],
        compiler_params=_cparams(("arbitrary", "arbitrary")),
        name="matmul_wstat_" + (act or "lin") + ("_res" if residual is not None else ""),
    )(*args)
    return res if len(res) > 1 else res[0]


def _split3(x):
    hi = x.astype(BF16)
    r1 = x - hi.astype(F32)
    mid = r1.astype(BF16)
    lo = (r1 - mid.astype(F32)).astype(BF16)
    return hi, mid, lo


def _dot_exact_lhs(x, sel):
    hi, mid, lo = _split3(x)
    d = functools.partial(jnp.dot, preferred_element_type=F32)
    return d(hi, sel) + d(mid, sel) + d(lo, sel)


def _dot_exact_rhs(sel, x):
    hi, mid, lo = _split3(x)
    d = functools.partial(jnp.dot, preferred_element_type=F32)
    return d(sel, hi) + d(sel, mid) + d(sel, lo)


def _gelu_tanh(x):
    c = math.sqrt(2.0 / math.pi)
    return 0.5 * x * (1.0 + jnp.tanh(c * (x + 0.044715 * (x * x * x))))


def _s5_kernel(u_ref, perm_ref, permt_ref, bre_ref, bim_ref, cre_ref, cim_ref, are_ref, aim_ref,
               apr_ref, api_ref, d_ref, wglu_ref, nw_ref, o_ref, sr_ref, si_ref, y_ref, st_re, st_im, *, tc):
    nsub = SUBLANES
    m = tc // nsub

    @pl.when(pl.program_id(1) == 0)
    def _():
        st_re[...] = jnp.zeros_like(st_re)
        st_im[...] = jnp.zeros_like(st_im)

    up = jnp.dot(perm_ref[...], u_ref[...], preferred_element_type=F32).astype(BF16)

    cw = S5_LANE_CHUNK
    uw = cw // S5_STATE * S5_CH_PER_GROUP
    sub = lax.broadcasted_iota(jnp.int32, (nsub, cw), 0)
    for j in range(S5_CHUNKS):
        cols = slice(j * cw, (j + 1) * cw)
        sr, si = sr_ref.at[j % 2], si_ref.at[j % 2]
        uj = up[:, j * uw:(j + 1) * uw]
        sr[...] = jnp.dot(uj, bre_ref[j], preferred_element_type=F32)
        si[...] = jnp.dot(uj, bim_ref[j], preferred_element_type=F32)
        ar = jnp.broadcast_to(are_ref[:, cols], (nsub, cw))
        ai = jnp.broadcast_to(aim_ref[:, cols], (nsub, cw))

        loc_r = loc_i = jnp.zeros((nsub, cw), F32)
        for t in range(m):
            rows = slice(t * nsub, (t + 1) * nsub)
            loc_r, loc_i = (ar * loc_r - ai * loc_i + sr[rows, :], ar * loc_i + ai * loc_r + si[rows, :])
            sr[rows, :] = loc_r
            si[rows, :] = loc_i

        er, ei = st_re[:, cols], st_im[:, cols]
        amr, ami = apr_ref[m - 1:m, cols], api_ref[m - 1:m, cols]
        ent_r = ent_i = jnp.zeros((nsub, cw), F32)
        for q in range(nsub):
            ent_r = jnp.where(sub == q, er, ent_r)
            ent_i = jnp.where(sub == q, ei, ent_i)
            er, ei = (amr * er - ami * ei + loc_r[q:q + 1, :], amr * ei + ami * er + loc_i[q:q + 1, :])
        st_re[:, cols] = er
        st_im[:, cols] = ei

        for t in range(m):
            rows = slice(t * nsub, (t + 1) * nsub)
            pr = apr_ref[t:t + 1, cols]
            pi = api_ref[t:t + 1, cols]
            sr[rows, :] = sr[rows, :] + (pr * ent_r - pi * ent_i)
            si[rows, :] = si[rows, :] + (pr * ent_i + pi * ent_r)
        y_ref[:, j * uw:(j + 1) * uw] = (
            jnp.dot(sr[...].astype(BF16), cre_ref[j], preferred_element_type=F32)
            - jnp.dot(si[...].astype(BF16), cim_ref[j], preferred_element_type=F32))

    y = _gelu_tanh(y_ref[...] + d_ref[...] * up.astype(F32))
    gate = jnp.dot(y.astype(BF16), wglu_ref[...], preferred_element_type=F32)
    out = y * jax.nn.sigmoid(gate)
    ms = jnp.mean(out * out, axis=-1, keepdims=True)
    outp = (out * lax.rsqrt(ms + EPS) * nw_ref[...]).astype(BF16)
    o_ref[...] = jnp.dot(permt_ref[...], outp, preferred_element_type=F32).astype(o_ref.dtype)


def _s5_discretise(lam_re, lam_im, log_dt, b_re, b_im):
    dt = jnp.exp(log_dt.astype(F32))[:, None]
    lr, li = lam_re.astype(F32), lam_im.astype(F32)
    mag = jnp.exp(lr * dt)
    ab_re = mag * jnp.cos(li * dt)
    ab_im = mag * jnp.sin(li * dt)
    den = lr * lr + li * li
    f_re = ((ab_re - 1.0) * lr + ab_im * li) / den
    f_im = (ab_im * lr - (ab_re - 1.0) * li) / den
    br, bi = b_re.astype(F32), b_im.astype(F32)
    bb_re = f_re[..., None] * br - f_im[..., None] * bi
    bb_im = f_re[..., None] * bi + f_im[..., None] * br
    return ab_re, ab_im, bb_re, bb_im


def _block_diag(blocks, per):
    g, r, c = blocks.shape
    b = blocks.reshape(g // per, per, r, c)
    eye = jnp.eye(per, dtype=blocks.dtype)
    return jnp.einsum("nirc,ij->nirjc", b, eye).reshape(g // per, per * r, per * c)


def s5_mixer(proj, bsz, seq, lam_re, lam_im, log_dt, b_re, b_im, c_re, c_im, d_skip, w_glu, norm_w, tc=512):
    per = S5_LANE_CHUNK // S5_STATE
    ab_re, ab_im, bb_re, bb_im = _s5_discretise(lam_re, lam_im, log_dt, b_re, b_im)
    bre = _block_diag(jnp.swapaxes(bb_re, 1, 2), per).astype(BF16)
    bim = _block_diag(jnp.swapaxes(bb_im, 1, 2), per).astype(BF16)
    cre = _block_diag(jnp.swapaxes(c_re.astype(F32), 1, 2), per).astype(BF16)
    cim = _block_diag(jnp.swapaxes(c_im.astype(F32), 1, 2), per).astype(BF16)
    nstate = ab_re.size
    tc = min(tc, seq)
    nct = seq // tc
    m = tc // SUBLANES
    steps = jnp.arange(1, m + 1, dtype=F32)[:, None, None]
    dt = jnp.exp(log_dt.astype(F32))[None, :, None]
    pow_mag = jnp.exp(steps * lam_re.astype(F32)[None] * dt)
    pow_ang = steps * lam_im.astype(F32)[None] * dt
    apow_re = (pow_mag * jnp.cos(pow_ang)).reshape(m, nstate)
    apow_im = (pow_mag * jnp.sin(pow_ang)).reshape(m, nstate)
    r = jnp.arange(tc)
    perm = (r[None, :] == ((r % SUBLANES) * m + r // SUBLANES)[:, None]).astype(BF16)
    uw = per * S5_CH_PER_GROUP
    const2 = lambda b, c: (0, 0)
    const3 = lambda b, c: (0, 0, 0)
    return pl.pallas_call(
        functools.partial(_s5_kernel, tc=tc),
        grid=(bsz, nct),
        in_specs=[
            pl.BlockSpec((tc, GROUP_WIDTH), lambda b, c: (b * nct + c, COL_S5)),
            pl.BlockSpec((tc, tc), const2),
            pl.BlockSpec((tc, tc), const2),
            pl.BlockSpec((S5_CHUNKS, uw, S5_LANE_CHUNK), const3),
            pl.BlockSpec((S5_CHUNKS, uw, S5_LANE_CHUNK), const3),
            pl.BlockSpec((S5_CHUNKS, S5_LANE_CHUNK, uw), const3),
            pl.BlockSpec((S5_CHUNKS, S5_LANE_CHUNK, uw), const3),
            pl.BlockSpec((1, nstate), const2),
            pl.BlockSpec((1, nstate), const2),
            pl.BlockSpec((m, nstate), const2),
            pl.BlockSpec((m, nstate), const2),
            pl.BlockSpec((1, GROUP_WIDTH), const2),
            pl.BlockSpec((GROUP_WIDTH, GROUP_WIDTH), const2),
            pl.BlockSpec((1, GROUP_WIDTH), const2),
        ],
        out_specs=pl.BlockSpec((tc, GROUP_WIDTH), lambda b, c: (b * nct + c, 0)),
        out_shape=jax.ShapeDtypeStruct((bsz * seq, GROUP_WIDTH), BF16),
        scratch_shapes=[
            pltpu.VMEM((2, tc, S5_LANE_CHUNK), F32),
            pltpu.VMEM((2, tc, S5_LANE_CHUNK), F32),
            pltpu.VMEM((tc, GROUP_WIDTH), F32),
            pltpu.VMEM((1, nstate), F32),
            pltpu.VMEM((1, nstate), F32),
        ],
        compiler_params=_cparams(("parallel", "arbitrary")),
        name="s5_mixer",
    )(proj, perm, perm.T, bre, bim, cre, cim, ab_re.reshape(1, nstate), ab_im.reshape(1, nstate),
      apow_re, apow_im, d_skip.reshape(1, GROUP_WIDTH).astype(F32), w_glu.astype(BF16), norm_w.reshape(1, GROUP_WIDTH).astype(F32))


def _rel_bucket(dist):
    n = jnp.maximum(dist, 0)
    max_exact = REL_BUCKETS // 2
    log_ratio = jnp.log(jnp.maximum(n, 1).astype(F32) / max_exact) / math.log(REL_MAX_DIST / max_exact)
    large = max_exact + (log_ratio * (REL_BUCKETS - max_exact)).astype(jnp.int32)
    large = jnp.minimum(large, REL_BUCKETS - 1)
    return jnp.where(n < max_exact, n, large)


def _bias_tiles(tbl, blk):
    assert blk >= REL_MAX_DIST
    i = jnp.arange(blk)[:, None]
    j = jnp.arange(blk)[None, :]
    buckets = jnp.stack([_rel_bucket(i - j), _rel_bucket(blk + i - j), _rel_bucket(jnp.full((blk, blk), 2 * blk))])
    onehot = (buckets[..., None] == jnp.arange(REL_BUCKETS)).astype(F32)
    tiles = jnp.einsum("tijk,hk->htij", onehot, tbl, precision=lax.Precision.HIGHEST) * LOG2E
    visible = jnp.stack([i >= j, jnp.ones((blk, blk), bool), jnp.ones((blk, blk), bool)])
    return jnp.where(visible[None], tiles, NEG_INF).astype(F32)


def _qk(q, k):
    return lax.dot_general(q, k, (((1,), (1,)), ((), ())), preferred_element_type=F32)


def _attn_kernel(qa_ref, qb_ref, k_ref, v_ref, bias_ref, p0_ref, p1_ref, o_hbm,
                 qs_ref, s_ref, mx_ref, ls_ref, acc_ref, kmean_ref, out_ref, stage_ref, sem,
                 *, mode, scale, heads, group, nq):
    blk = ATT_BLOCK
    bi = pl.program_id(0)
    i = pl.program_id(1)
    owns = (i, nq - 1 - i)
    q_refs = (qa_ref, qb_ref)
    lane = lax.broadcasted_iota(jnp.int32, (blk, LANES), 1)
    n_maps = 2 if mode == "diff" else 1
    nb = k_ref.shape[0] // blk
    reps = blk // LANES
    n_steps = nq + 1

    def rows(side):
        if isinstance(side, int):
            return slice(side * blk, (side + 1) * blk)
        return pl.ds(pl.multiple_of(side * blk, blk), blk)

    chunk = max(c for c in range(1, ATT_STEP_UNROLL + 1) if n_steps % c == 0)

    def run_steps(step):
        def body(o, carry):
            for c in range(chunk):
                step(o * chunk + c)
            return carry
        lax.fori_loop(0, n_steps // chunk, body, 0)

    def step_info(t):
        first = t <= i
        side = jnp.where(first, 0, 1)
        n = jnp.where(first, t, t - (i + 1))
        own = jnp.where(first, owns[0], owns[1])
        return side, n, jnp.minimum(own - n, 2)

    if mode == "moba":
        @pl.when(i == 0)
        def _():
            for h in range(heads):
                kf = k_ref[:, h * LANES:(h + 1) * LANES].astype(F32).reshape(nb, blk, LANES)
                kmean = jnp.sum(kf, axis=1) * (1.0 / blk)
                kmean_ref[h] = jnp.concatenate([kmean, jnp.zeros((GATE_ROWS - nb, LANES), F32)], axis=0)

    for g0 in range(0, heads, group):
        maps = [(h, mi) for h in range(g0, g0 + group) for mi in range(n_maps)]
        for side in range(2):
            own = owns[side]
            for h in range(g0, g0 + group):
                hs = slice(h * LANES, (h + 1) * LANES)
                q = q_refs[side][:, hs]
                if mode == "diff":
                    qf = q.astype(F32)
                    qs_ref[(h - g0) * 2, rows(side), :] = jnp.where(lane < DIFF_HEAD_DIM, qf, 0.0).astype(BF16)
                    qs_ref[(h - g0) * 2 + 1, rows(side), :] = jnp.where(lane >= DIFF_HEAD_DIM, qf, 0.0).astype(BF16)
                else:
                    hi, mid, lo = _split3(kmean_ref[h])
                    gate = _qk(hi, q) + _qk(mid, q) + _qk(lo, q)
                    sub = lax.broadcasted_iota(jnp.int32, (GATE_ROWS, blk), 0)
                    gate = jnp.where(sub < own, gate, NEG_INF)
                    penalty = jnp.zeros((GATE_ROWS, blk), F32)
                    for n in range(nb):
                        row = gate[n:n + 1, :]
                        beats = jnp.logical_and(
                            jnp.logical_or(gate > row, jnp.logical_and(gate == row, sub < n)), sub < nb)
                        rank = jnp.sum(beats.astype(F32), axis=0, keepdims=True)
                        keep = jnp.logical_or(jnp.logical_and(rank < MOBA_TOPK, n < own), n == own)
                        penalty = jnp.where(sub == n, jnp.where(keep, 0.0, NEG_INF), penalty)
                    pen_t = jnp.transpose(
                        jnp.concatenate([penalty, jnp.zeros((LANES - GATE_ROWS, blk), F32)], axis=0))
                    qs_ref[h - g0, rows(side), :] = jnp.concatenate([q, pen_t.astype(BF16)], axis=1)
            for idx in range(len(maps)):
                mx_ref[idx, rows(side), :] = jnp.full((blk, LANES), -jnp.inf, F32)

        def logits_step(t, maps=maps):
            side, n, tile_dist = step_info(t)
            start = pl.multiple_of(n * blk, blk)
            if mode == "moba":
                onehot = jnp.where(lane == n, 1.0, 0.0).astype(BF16)
            for idx, (h, mi) in enumerate(maps):
                hs = slice(h * LANES, (h + 1) * LANES)
                kn = k_ref[pl.ds(start, blk), hs]
                if mode == "moba":
                    kn = jnp.concatenate([kn, onehot], axis=1)
                s = _qk(qs_ref[idx, rows(side), :], kn) * (scale * LOG2E) + bias_ref[h, tile_dist]
                s_ref[idx, t] = s
                part = s[:, :LANES]
                for r in range(1, reps):
                    part = jnp.maximum(part, s[:, r * LANES:(r + 1) * LANES])
                mx_ref[idx, rows(side), :] = jnp.maximum(mx_ref[idx, rows(side), :], part)

        run_steps(logits_step)

        for side in range(2):
            for idx in range(len(maps)):
                mx_ref[idx, rows(side), :] = jnp.broadcast_to(
                    jnp.max(mx_ref[idx, rows(side), :], axis=-1, keepdims=True), (blk, LANES))
                ls_ref[idx, rows(side), :] = jnp.zeros((blk, LANES), F32)
                acc_ref[idx, rows(side), :] = jnp.zeros((blk, LANES), F32)

        def value_step(t, maps=maps):
            side, n, _ = step_info(t)
            start = pl.multiple_of(n * blk, blk)
            for idx, (h, mi) in enumerate(maps):
                hs = slice(h * LANES, (h + 1) * LANES)
                s = s_ref[idx, t]
                mb = mx_ref[idx, rows(side), :]
                ps = [jnp.exp2(s[:, r * LANES:(r + 1) * LANES] - mb) for r in range(reps)]
                tot = ps[0]
                for r in range(1, reps):
                    tot = tot + ps[r]
                ls_ref[idx, rows(side), :] += tot
                p = jnp.concatenate(ps, axis=1).astype(BF16)
                acc_ref[idx, rows(side), :] += jnp.dot(p, v_ref[pl.ds(start, blk), hs], preferred_element_type=F32)

        run_steps(value_step)

        for side in range(2):
            for h in range(g0, g0 + group):
                hs = slice(h * LANES, (h + 1) * LANES)
                i0 = (h - g0) * n_maps
                o = acc_ref[i0, rows(side), :] / jnp.sum(ls_ref[i0, rows(side), :], axis=-1, keepdims=True)
                if mode == "diff":
                    o2 = (acc_ref[i0 + 1, rows(side), :]
                          / jnp.sum(ls_ref[i0 + 1, rows(side), :], axis=-1, keepdims=True))
                    o = o - p0_ref[...] * o2
                    ms = jnp.mean(o * o, axis=-1, keepdims=True)
                    stage_ref[side, :, hs] = (o * lax.rsqrt(ms + EPS) * p1_ref[...]).astype(stage_ref.dtype)
                else:
                    out_ref[side, :, hs] = o

    copies = []
    for side in range(2):
        if mode == "moba":
            o = out_ref[side]
            ms = jnp.mean(o * o, axis=-1, keepdims=True)
            stage_ref[side] = (o * lax.rsqrt(ms + EPS) * p1_ref[...]).astype(stage_ref.dtype)
        row0 = pl.multiple_of((bi * nq + owns[side]) * blk, blk)
        copies.append(pltpu.make_async_copy(stage_ref.at[side], o_hbm.at[pl.ds(row0, blk), :], sem.at[side]))
        copies[-1].start()
    for cp in copies:
        cp.wait()


def attention(proj, bsz, seq, col_q, col_k, col_v, bias, p0, p1, mode, scale, heads):
    blk = ATT_BLOCK
    nq = seq // blk
    width = heads * LANES
    assert width == GROUP_WIDTH and seq % blk == 0 and nq % 2 == 0
    nb = seq // blk
    assert nb <= GATE_ROWS
    n_maps = 2 if mode == "diff" else 1
    group = ATT_MAPS_PER_PASS // n_maps
    const2 = lambda b, i: (0, 0)
    once = pl.Buffered(1)
    return pl.pallas_call(
        functools.partial(_attn_kernel, mode=mode, scale=scale, heads=heads, group=group, nq=nq),
        grid=(bsz, nq // 2),
        in_specs=[
            pl.BlockSpec((blk, width), lambda b, i: (b * nq + i, col_q)),
            pl.BlockSpec((blk, width), lambda b, i: (b * nq + nq - 1 - i, col_q)),
            pl.BlockSpec((seq, width), lambda b, i: (b, col_k)),
            pl.BlockSpec((seq, width), lambda b, i: (b, col_v)),
            pl.BlockSpec((heads, 3, blk, blk), lambda b, i: (0, 0, 0, 0), pipeline_mode=once),
            pl.BlockSpec(p0.shape, const2, pipeline_mode=once),
            pl.BlockSpec(p1.shape, const2, pipeline_mode=once),
        ],
        out_specs=pl.BlockSpec(memory_space=pl.ANY),
        out_shape=jax.ShapeDtypeStruct((bsz * seq, width), BF16),
        scratch_shapes=[
            pltpu.VMEM((ATT_MAPS_PER_PASS, 2 * blk, LANES * (1 if mode == "diff" else 2)), BF16),
            pltpu.VMEM((ATT_MAPS_PER_PASS, nq + 1, blk, blk), F32),
            pltpu.VMEM((ATT_MAPS_PER_PASS, 2 * blk, LANES), F32),
            pltpu.VMEM((ATT_MAPS_PER_PASS, 2 * blk, LANES), F32),
            pltpu.VMEM((ATT_MAPS_PER_PASS, 2 * blk, LANES), F32),
            pltpu.VMEM((heads, GATE_ROWS, LANES), F32),
            pltpu.VMEM((2, blk, width), F32),
            pltpu.VMEM((2, blk, width), BF16),
            pltpu.SemaphoreType.DMA((2,)),
        ],
        compiler_params=_cparams(("parallel", "arbitrary")),
        name="attn_" + mode,
    )(proj, proj, proj, proj, bias, p0, p1)


def _softplus(x):
    return jnp.maximum(x, 0.0) + jnp.log(1.0 + jnp.exp(-jnp.abs(x)))


def _silu(x):
    return x * jax.nn.sigmoid(x)


def _ssd_kernel(z_ref, xbc_ref, dt_ref, dtt_ref, cw_ref, cb_ref, dtb_ref, dtbt_ref, a_ref, at_ref, dsk_ref,
                nw_ref, o_ref, ext_ref, st_ref, y_ref, *, lc):
    halo = SSD_HALO
    c = pl.program_id(1)

    @pl.when(c == 0)
    def _():
        ext_ref[0:halo, :] = jnp.zeros((halo, SSD_XBC), F32)
        st_ref[...] = jnp.zeros_like(st_ref)

    ext_ref[halo:halo + lc, :] = xbc_ref[...].astype(F32)
    conv = cb_ref[...] + cw_ref[SSD_CONV - 1:SSD_CONV, :] * ext_ref[halo:halo + lc, :]
    for kk in range(1, SSD_CONV):
        conv = conv + cw_ref[SSD_CONV - 1 - kk:SSD_CONV - kk, :] * ext_ref[halo - kk:halo - kk + lc, :]
    ext_ref[0:halo, :] = ext_ref[lc:lc + halo, :]
    xbc = _silu(conv)
    xs = xbc[:, :GROUP_WIDTH]

    dt_c = _softplus(dt_ref[...] + dtb_ref[...])
    dt_r = _softplus(dtt_ref[...] + dtbt_ref[...])
    a_c = dt_c * a_ref[...]
    a_r = dt_r * at_ref[...]
    row = lax.broadcasted_iota(jnp.int32, (lc, lc), 0)
    colm = lax.broadcasted_iota(jnp.int32, (lc, lc), 1)
    causal = row >= colm
    tri = causal.astype(BF16)
    tri_t = (colm >= row).astype(BF16)
    acum_c = _dot_exact_rhs(tri, a_c)
    acum_r = _dot_exact_lhs(a_r, tri_t)
    alast_c = acum_c[lc - 1:lc, :]

    lane = lax.broadcasted_iota(jnp.int32, (lc, LANES), 1)
    lo_half = lane < SSD_HEAD_DIM
    heads_per_group = SSD_HEADS // SSD_GROUPS
    for g in range(SSD_GROUPS):
        bg = xbc[:, GROUP_WIDTH + g * SSD_STATE:GROUP_WIDTH + (g + 1) * SSD_STATE].astype(BF16)
        cg = xbc[:, GROUP_WIDTH + SSD_BC + g * SSD_STATE:GROUP_WIDTH + SSD_BC + (g + 1) * SSD_STATE].astype(BF16)
        cb = _qk(cg, bg)
        for pr in range(heads_per_group // 2):
            ha = g * heads_per_group + 2 * pr
            hb = ha + 1
            tile = slice(ha * SSD_HEAD_DIM, (ha + 2) * SSD_HEAD_DIM)
            xp = xs[:, tile]

            def per_head(col_a, col_b):
                return jnp.where(lo_half, col_a, col_b)

            xdt = xp * per_head(dt_c[:, ha:ha + 1], dt_c[:, hb:hb + 1])
            ydiag = None
            for hh, keep in ((ha, lo_half), (hb, jnp.logical_not(lo_half))):
                seg = acum_c[:, hh:hh + 1] - acum_r[hh:hh + 1, :]
                decay = jnp.exp(jnp.where(causal, seg, NEG_INF))
                mm = (cb * decay).astype(BF16)
                part = jnp.dot(mm, jnp.where(keep, xdt, 0.0).astype(BF16), preferred_element_type=F32)
                ydiag = part if ydiag is None else ydiag + part
            st = st_ref[ha // 2]
            yoff = _qk(cg, st.astype(BF16)) * per_head(jnp.exp(acum_c[:, ha:ha + 1]), jnp.exp(acum_c[:, hb:hb + 1]))
            y_ref[:, tile] = ydiag + yoff + dsk_ref[:, tile] * xp
            to_end = per_head(jnp.exp(alast_c[:, ha:ha + 1] - acum_c[:, ha:ha + 1]),
                              jnp.exp(alast_c[:, hb:hb + 1] - acum_c[:, hb:hb + 1]))
            xdec_t = jnp.transpose(xdt * to_end).astype(BF16)
            sub = lax.broadcasted_iota(jnp.int32, (LANES, SSD_STATE), 0)
            chunk_decay = jnp.where(sub < SSD_HEAD_DIM, jnp.exp(alast_c[:, ha:ha + 1]),
                                    jnp.exp(alast_c[:, hb:hb + 1]))
            st_ref[ha // 2] = st * chunk_decay + jnp.dot(xdec_t, bg, preferred_element_type=F32)

    y = y_ref[...] * _silu(z_ref[...].astype(F32))
    gw = GROUP_WIDTH // SSD_GROUPS
    for g in range(SSD_GROUPS):
        yg = y[:, g * gw:(g + 1) * gw]
        ms = jnp.mean(yg * yg, axis=-1, keepdims=True)
        o_ref[:, g * gw:(g + 1) * gw] = (yg * lax.rsqrt(ms + EPS) * nw_ref[:, g * gw:(g + 1) * gw]).astype(o_ref.dtype)


def _pad_lanes(v):
    return jnp.pad(v.astype(F32), (0, LANES - v.shape[0])).reshape(1, LANES)


def ssd_mixer(proj, dt_raw, bsz, seq, conv_w, conv_b, dt_bias, a_log, d_skip, norm_w, lc=128):
    nc = seq // lc
    t = bsz * seq
    a = -jnp.exp(a_log.astype(F32))
    dt_t = jnp.transpose(dt_raw[:, :SSD_HEADS])
    col16 = lambda v: jnp.broadcast_to(v.astype(F32)[:, None], (SSD_HEADS, LANES))
    dskip = jnp.repeat(d_skip.astype(F32), SSD_HEAD_DIM).reshape(1, GROUP_WIDTH)
    const2 = lambda b, c: (0, 0)
    return pl.pallas_call(
        functools.partial(_ssd_kernel, lc=lc),
        grid=(bsz, nc),
        in_specs=[
            pl.BlockSpec((lc, GROUP_WIDTH), lambda b, c: (b * nc + c, COL_Z)),
            pl.BlockSpec((lc, SSD_XBC), lambda b, c: (b * nc + c, COL_XBC)),
            pl.BlockSpec((lc, LANES), lambda b, c: (b * nc + c, 0)),
            pl.BlockSpec((SSD_HEADS, lc), lambda b, c: (0, b * nc + c)),
            pl.BlockSpec((SSD_CONV, SSD_XBC), const2),
            pl.BlockSpec((1, SSD_XBC), const2),
            pl.BlockSpec((1, LANES), const2),
            pl.BlockSpec((SSD_HEADS, LANES), const2),
            pl.BlockSpec((1, LANES), const2),
            pl.BlockSpec((SSD_HEADS, LANES), const2),
            pl.BlockSpec((1, GROUP_WIDTH), const2),
            pl.BlockSpec((1, GROUP_WIDTH), const2),
        ],
        out_specs=pl.BlockSpec((lc, GROUP_WIDTH), lambda b, c: (b * nc + c, 0)),
        out_shape=jax.ShapeDtypeStruct((t, GROUP_WIDTH), BF16),
        scratch_shapes=[
            pltpu.VMEM((lc + SSD_HALO, SSD_XBC), F32),
            pltpu.VMEM((SSD_HEADS // 2, 2 * SSD_HEAD_DIM, SSD_STATE), F32),
            pltpu.VMEM((lc, GROUP_WIDTH), F32),
        ],
        compiler_params=_cparams(("parallel", "arbitrary")),
        name="ssd_mixer",
    )(proj, proj, dt_raw, dt_t, conv_w.reshape(SSD_CONV, SSD_XBC).astype(F32),
      conv_b.reshape(1, SSD_XBC).astype(F32), _pad_lanes(dt_bias), col16(dt_bias), _pad_lanes(a), col16(a),
      dskip, norm_w.reshape(1, GROUP_WIDTH).astype(F32))


def kernel(x, rel_bias_table, attn_norm_w, w_in, s5_lam_re, s5_lam_im, s5_log_dt, s5_b_re, s5_b_im, s5_c_re, s5_c_im, s5_d, s5_w_glu, s5_out_norm_w, diff_lam_q1, diff_lam_k1, diff_lam_q2, diff_lam_k2, diff_subln_w, moba_out_norm_w, ssd_conv_w, ssd_conv_b, ssd_dt_bias, ssd_a_log, ssd_d, ssd_norm_w, w_out, mlp_norm_w, w_up, w_down, final_norm_w):
    bsz, seq, d_model = x.shape
    depth = w_in.shape[0]
    t = bsz * seq
    x = x.reshape(t, d_model).astype(F32)

    w_in_t = jnp.swapaxes(w_in, 1, 2)
    w_dt_b = jnp.pad(w_in[:, :, PROJ_MAIN:], ((0, 0), (0, 0), (0, LANES - SSD_HEADS))).astype(BF16)

    tbl = rel_bias_table.astype(F32).T
    bias_diff = _bias_tiles(tbl[:DIFF_HEADS], ATT_BLOCK)
    bias_moba = _bias_tiles(tbl[DIFF_HEADS:], ATT_BLOCK)

    def row_scale(ssq):
        r = lax.rsqrt(jnp.sum(ssq, axis=(0, 2)) * (1.0 / d_model) + EPS)
        return jnp.broadcast_to(r[:, None], (t, LANES))

    h, scale = rmsnorm(x, attn_norm_w[0], BF16), None
    for l in range(depth):
        proj = matmul_wstat([h], w_in_t, l, PROJ_MAIN, BF16, row_scale=scale, w_transposed=True)
        dt_raw = matmul(h, w_dt_b, l, LANES, F32, row_scale=scale)

        y_s5 = s5_mixer(proj, bsz, seq, s5_lam_re[l], s5_lam_im[l], s5_log_dt[l], s5_b_re[l], s5_b_im[l],
                        s5_c_re[l], s5_c_im[l], s5_d[l], s5_w_glu[l], s5_out_norm_w[l])

        lam_init = 0.8 - 0.6 * math.exp(-0.3 * l)
        lam = (jnp.exp(jnp.sum(diff_lam_q1[l].astype(F32) * diff_lam_k1[l].astype(F32)))
               - jnp.exp(jnp.sum(diff_lam_q2[l].astype(F32) * diff_lam_k2[l].astype(F32))) + lam_init)
        y_diff = attention(proj, bsz, seq, COL_DQ, COL_DK, COL_DV, bias_diff,
                           jnp.broadcast_to(lam, (1, LANES)).astype(F32),
                           (diff_subln_w[l].astype(F32) * (1.0 - lam_init)).reshape(1, LANES),
                           "diff", DIFF_HEAD_DIM ** -0.5, DIFF_HEADS)
        y_moba = attention(proj, bsz, seq, COL_MQ, COL_MK, COL_MV, bias_moba,
                           jnp.zeros((1, LANES), F32),
                           moba_out_norm_w[l].astype(F32).reshape(1, GROUP_WIDTH),
                           "moba", (GROUP_WIDTH // MOBA_HEADS) ** -0.5, MOBA_HEADS)
        y_ssd = ssd_mixer(proj, dt_raw, bsz, seq, ssd_conv_w[l], ssd_conv_b[l], ssd_dt_bias[l], ssd_a_log[l],
                          ssd_d[l], ssd_norm_w[l])

        x, h, ssq = matmul_wstat([y_s5, y_diff, y_moba, y_ssd], w_out, l, d_model, F32, residual=x,
                                 norm_w=mlp_norm_w[l])
        u, w_down_b = matmul_wstat([h], w_up, l, w_up.shape[2], BF16, act="relu2", row_scale=row_scale(ssq),
                                   cast_through=w_down)
        if l + 1 < depth:
            x, h, ssq = matmul(u, w_down_b[None], 0, d_model, F32, residual=x, norm_w=attn_norm_w[l + 1])
            scale = row_scale(ssq)
        else:
            x = matmul(u, w_down_b[None], 0, d_model, F32, residual=x)

    return rmsnorm(x, final_norm_w, F32).reshape(bsz, seq, d_model)
```

```python
import functools
import math

import jax
import jax.numpy as jnp
from jax import lax
from jax.experimental import pallas as pl
from jax.experimental.pallas import tpu as pltpu

F32 = jnp.float32
BF16 = jnp.bfloat16
EPS = 1e-6
NEG_INF = -1e30

GROUP_WIDTH = 1024
S5_CH_PER_GROUP = 16
S5_STATE = 64
DIFF_HEADS = 8
DIFF_HEAD_DIM = 64
MOBA_HEADS = 8
MOBA_BLOCK = 256
MOBA_TOPK = 3
SSD_HEAD_DIM = 64
SSD_HEADS = 16
SSD_GROUPS = 4
SSD_STATE = 128
SSD_CONV = 4
SSD_HALO = 8
SSD_BC = SSD_GROUPS * SSD_STATE
SSD_XBC = GROUP_WIDTH + 2 * SSD_BC
REL_BUCKETS = 32
REL_MAX_DIST = 128

COL_S5 = 0
COL_DQ, COL_DK, COL_DV = 1, 2, 3
COL_MQ, COL_MK, COL_MV = 4, 5, 6
COL_Z = 7
COL_XBC = 4
PROJ_MAIN = 10 * GROUP_WIDTH

LANES = 128
SUBLANES = 8
ATT_BLOCK = 256
ATT_MAPS_PER_PASS = 8
ATT_STEP_UNROLL = 3
GATE_ROWS = 16
LOG2E = math.log2(math.e)
S5_LANE_CHUNK = 512
S5_CHUNKS = (GROUP_WIDTH // S5_CH_PER_GROUP) * S5_STATE // S5_LANE_CHUNK
WSTAT_PIECES = 4
VMEM_LIMIT = 56 * 1024 * 1024


def _cparams(sem):
    return pltpu.CompilerParams(dimension_semantics=sem, vmem_limit_bytes=VMEM_LIMIT)


def _rmsnorm_kernel(x_ref, w_ref, o_ref):
    x = x_ref[...].astype(F32)
    ms = jnp.mean(x * x, axis=-1, keepdims=True)
    o_ref[...] = (x * lax.rsqrt(ms + EPS) * w_ref[...]).astype(o_ref.dtype)


def rmsnorm(x, w, out_dtype, tm=256):
    t, d = x.shape
    return pl.pallas_call(
        _rmsnorm_kernel,
        grid=(t // tm,),
        in_specs=[pl.BlockSpec((tm, d), lambda i: (i, 0)), pl.BlockSpec((1, d), lambda i: (0, 0))],
        out_specs=pl.BlockSpec((tm, d), lambda i: (i, 0)),
        out_shape=jax.ShapeDtypeStruct((t, d), out_dtype),
        compiler_params=_cparams(("parallel",)),
        name="rmsnorm",
    )(x, w.reshape(1, d).astype(F32))


def _split_refs(refs, n_lhs, n_w, has_res, has_scale, has_norm, has_cast):
    it = iter(refs)
    take = lambda n: [next(it) for _ in range(n)]
    r = {"lhs": take(n_lhs), "w": take(n_w)}
    r["res"] = next(it) if has_res else None
    r["scale"] = next(it) if has_scale else None
    r["normw"] = next(it) if has_norm else None
    r["cast_in"] = next(it) if has_cast else None
    r["out"] = next(it)
    r["xw"], r["ssq"] = (next(it), next(it)) if has_norm else (None, None)
    r["cast_out"] = next(it) if has_cast else None
    r["scratch"] = list(it)
    return r


def _mm_epilogue(acc, r, act):
    reps = acc.shape[1] // LANES
    if r["scale"] is not None:
        acc = acc * jnp.concatenate([r["scale"][...]] * reps, axis=1)
    if act == "relu2":
        p = jnp.maximum(acc, 0.0)
        acc = p * p
    if r["res"] is not None:
        acc = r["res"][...] + acc
    r["out"][...] = acc.astype(r["out"].dtype)
    if r["normw"] is not None:
        r["xw"][...] = (acc * r["normw"][...]).astype(BF16)
        sq = acc * acc
        part = sq[:, :LANES]
        for c in range(1, reps):
            part = part + sq[:, c * LANES:(c + 1) * LANES]
        r["ssq"][...] = part


def _mm_kernel(*refs, nk, act, flags):
    r = _split_refs(refs, 1, 1, *flags)
    a_ref, w_ref, acc_ref = r["lhs"][0], r["w"][0], r["scratch"][0]
    k = pl.program_id(2)

    def partial_dot():
        return jnp.dot(a_ref[...], w_ref[...], preferred_element_type=F32)

    if nk == 1:
        _mm_epilogue(partial_dot(), r, act)
        return

    @pl.when(k == 0)
    def _():
        acc_ref[...] = partial_dot()

    @pl.when(jnp.logical_and(k > 0, k < nk - 1))
    def _():
        acc_ref[...] += partial_dot()

    @pl.when(k == nk - 1)
    def _():
        _mm_epilogue(acc_ref[...] + partial_dot(), r, act)


def _norm_out(m, n_out, tm, tn, nj, idx, idx3):
    specs = [pl.BlockSpec((tm, tn), idx), pl.BlockSpec((None, tm, LANES), idx3)]
    shapes = [jax.ShapeDtypeStruct((m, n_out), BF16), jax.ShapeDtypeStruct((nj, m, LANES), F32)]
    return specs, shapes


def matmul(lhs, w3, layer, n_out, out_dtype, act=None, residual=None, row_scale=None, norm_w=None,
           tm=1024, tn=1024, tk=2048):
    m, kdim = lhs.shape[0], w3.shape[1]
    tm, tn, tk = min(tm, m), min(tn, n_out), min(tk, kdim)
    nk = kdim // tk
    assert m % tm == 0 and n_out % tn == 0 and kdim % tk == 0
    tile = lambda i, j, k: (i, j)
    in_specs = [pl.BlockSpec((tm, tk), lambda i, j, k: (i, k)),
                pl.BlockSpec((None, tk, tn), lambda i, j, k: (layer, k, j))]
    args = [lhs, w3]
    if residual is not None:
        in_specs.append(pl.BlockSpec((tm, tn), tile))
        args.append(residual)
    if row_scale is not None:
        in_specs.append(pl.BlockSpec((tm, LANES), lambda i, j, k: (i, 0)))
        args.append(row_scale)
    out_specs = [pl.BlockSpec((tm, tn), tile)]
    out_shape = [jax.ShapeDtypeStruct((m, n_out), out_dtype)]
    if norm_w is not None:
        in_specs.append(pl.BlockSpec((1, tn), lambda i, j, k: (0, j)))
        args.append(norm_w.reshape(1, n_out).astype(F32))
        specs, shapes = _norm_out(m, n_out, tm, tn, n_out // tn, tile, lambda i, j, k: (j, i, 0))
        out_specs += specs
        out_shape += shapes
    flags = (residual is not None, row_scale is not None, norm_w is not None, False)
    res = pl.pallas_call(
        functools.partial(_mm_kernel, nk=nk, act=act, flags=flags),
        grid=(m // tm, n_out // tn, nk),
        in_specs=in_specs,
        out_specs=out_specs,
        out_shape=out_shape,
        scratch_shapes=[pltpu.VMEM((tm, tn) if nk > 1 else (8, LANES), F32)],
        compiler_params=_cparams(("parallel", "parallel", "arbitrary")),
        name="matmul_" + (act or "lin") + ("_res" if residual is not None else ""),
    )(*args)
    return res if norm_w is not None else res[0]


def _mm_wstat_kernel(*refs, act, n_lhs, flags, w_transposed):
    r = _split_refs(refs, n_lhs, WSTAT_PIECES, *flags)
    wb_ref = r["scratch"][0]
    kp = r["w"][0].shape[1 if w_transposed else 0]

    @pl.when(pl.program_id(1) == 0)
    def _():
        for p, w_ref in enumerate(r["w"]):
            if w_transposed:
                wb_ref[:, p * kp:(p + 1) * kp] = w_ref[...].astype(BF16)
            else:
                wb_ref[p * kp:(p + 1) * kp, :] = w_ref[...].astype(BF16)

    if r["cast_in"] is not None:
        r["cast_out"][...] = r["cast_in"][...].astype(BF16)

    kw = r["lhs"][0].shape[1]
    acc = None
    for i, a_ref in enumerate(r["lhs"]):
        if w_transposed:
            p = lax.dot_general(a_ref[...], wb_ref[:, i * kw:(i + 1) * kw], (((1,), (1,)), ((), ())),
                                preferred_element_type=F32)
        else:
            p = jnp.dot(a_ref[...], wb_ref[i * kw:(i + 1) * kw, :], preferred_element_type=F32)
        acc = p if acc is None else acc + p
    _mm_epilogue(acc, r, act)


def matmul_wstat(lhs, w3, layer, n_out, out_dtype, act=None, residual=None, row_scale=None, norm_w=None,
                 w_transposed=False, cast_through=None, tm=1024, tn=512):
    lhs = list(lhs)
    m = lhs[0].shape[0]
    kdim = w3.shape[2 if w_transposed else 1]
    assert sum(a.shape[1] for a in lhs) == kdim and all(a.shape[1] == lhs[0].shape[1] for a in lhs)
    assert m % tm == 0 and n_out % tn == 0 and kdim % WSTAT_PIECES == 0
    ni, nj, kp = m // tm, n_out // tn, kdim // WSTAT_PIECES
    assert ni > WSTAT_PIECES
    tile = lambda j, i: (i, j)
    in_specs = [pl.BlockSpec((tm, a.shape[1]), lambda j, i: (i, 0)) for a in lhs]
    for p in range(WSTAT_PIECES):
        col = lambda j, i, p=p: jnp.minimum(j + (i > p).astype(jnp.int32), nj - 1)
        if w_transposed:
            in_specs.append(pl.BlockSpec((None, tn, kp), lambda j, i, p=p, col=col: (layer, col(j, i), p)))
        else:
            in_specs.append(pl.BlockSpec((None, kp, tn), lambda j, i, p=p, col=col: (layer, p, col(j, i))))
    args = lhs + [w3] * WSTAT_PIECES
    if residual is not None:
        in_specs.append(pl.BlockSpec((tm, tn), tile))
        args.append(residual)
    if row_scale is not None:
        in_specs.append(pl.BlockSpec((tm, LANES), lambda j, i: (i, 0)))
        args.append(row_scale)
    if norm_w is not None:
        in_specs.append(pl.BlockSpec((1, tn), lambda j, i: (0, j)))
        args.append(norm_w.reshape(1, n_out).astype(F32))
    out_specs = [pl.BlockSpec((tm, tn), tile)]
    out_shape = [jax.ShapeDtypeStruct((m, n_out), out_dtype)]
    if norm_w is not None:
        specs, shapes = _norm_out(m, n_out, tm, tn, nj, tile, lambda j, i: (j, i, 0))
        out_specs += specs
        out_shape += shapes
    if cast_through is not None:
        steps = nj * ni
        rows, cols = cast_through.shape[1:]
        assert rows % steps == 0
        slab = rows // steps
        in_specs.append(pl.BlockSpec((None, slab, cols), lambda j, i: (layer, j * ni + i, 0)))
        args.append(cast_through)
        out_specs.append(pl.BlockSpec((slab, cols), lambda j, i: (j * ni + i, 0)))
        out_shape.append(jax.ShapeDtypeStruct((rows, cols), BF16))
    flags = (residual is not None, row_scale is not None, norm_w is not None, cast_through is not None)
    res = pl.pallas_call(
        functools.partial(_mm_wstat_kernel, act=act, n_lhs=len(lhs), flags=flags, w_transposed=w_transposed),
        grid=(nj, ni),
        in_specs=in_specs,
        out_specs=out_specs,
        out_shape=out_shape,
        scratch_shapes=[pltpu.VMEM((tn, kdim) if w_transposed else (kdim, tn), BF16)],
        compiler_params=_cparams(("arbitrary", "arbitrary")),
        name="matmul_wstat_" + (act or "lin") + ("_res" if residual is not None else ""),
    )(*args)
    return res if len(res) > 1 else res[0]


def _split3(x):
    hi = x.astype(BF16)
    r1 = x - hi.astype(F32)
    mid = r1.astype(BF16)
    lo = (r1 - mid.astype(F32)).astype(BF16)
    return hi, mid, lo


def _dot_exact_lhs(x, sel):
    hi, mid, lo = _split3(x)
    d = functools.partial(jnp.dot, preferred_element_type=F32)
    return d(hi, sel) + d(mid, sel) + d(lo, sel)


def _dot_exact_rhs(sel, x):
    hi, mid, lo = _split3(x)
    d = functools.partial(jnp.dot, preferred_element_type=F32)
    return d(sel, hi) + d(sel, mid) + d(sel, lo)


def _gelu_tanh(x):
    c = math.sqrt(2.0 / math.pi)
    return 0.5 * x * (1.0 + jnp.tanh(c * (x + 0.044715 * (x * x * x))))


def _s5_kernel(u_ref, perm_ref, permt_ref, bre_ref, bim_ref, cre_ref, cim_ref, are_ref, aim_ref,
               apr_ref, api_ref, d_ref, wglu_ref, nw_ref, o_ref, sr_ref, si_ref, y_ref, st_re, st_im, *, tc):
    nsub = SUBLANES
    m = tc // nsub

    @pl.when(pl.program_id(1) == 0)
    def _():
        st_re[...] = jnp.zeros_like(st_re)
        st_im[...] = jnp.zeros_like(st_im)

    up = jnp.dot(perm_ref[...], u_ref[...], preferred_element_type=F32).astype(BF16)

    cw = S5_LANE_CHUNK
    uw = cw // S5_STATE * S5_CH_PER_GROUP
    sub = lax.broadcasted_iota(jnp.int32, (nsub, cw), 0)
    for j in range(S5_CHUNKS):
        cols = slice(j * cw, (j + 1) * cw)
        sr, si = sr_ref.at[j % 2], si_ref.at[j % 2]
        uj = up[:, j * uw:(j + 1) * uw]
        sr[...] = jnp.dot(uj, bre_ref[j], preferred_element_type=F32)
        si[...] = jnp.dot(uj, bim_ref[j], preferred_element_type=F32)
        ar = jnp.broadcast_to(are_ref[:, cols], (nsub, cw))
        ai = jnp.broadcast_to(aim_ref[:, cols], (nsub, cw))

        loc_r = loc_i = jnp.zeros((nsub, cw), F32)
        for t in range(m):
            rows = slice(t * nsub, (t + 1) * nsub)
            loc_r, loc_i = (ar * loc_r - ai * loc_i + sr[rows, :], ar * loc_i + ai * loc_r + si[rows, :])
            sr[rows, :] = loc_r
            si[rows, :] = loc_i

        er, ei = st_re[:, cols], st_im[:, cols]
        amr, ami = apr_ref[m - 1:m, cols], api_ref[m - 1:m, cols]
        ent_r = ent_i = jnp.zeros((nsub, cw), F32)
        for q in range(nsub):
            ent_r = jnp.where(sub == q, er, ent_r)
            ent_i = jnp.where(sub == q, ei, ent_i)
            er, ei = (amr * er - ami * ei + loc_r[q:q + 1, :], amr * ei + ami * er + loc_i[q:q + 1, :])
        st_re[:, cols] = er
        st_im[:, cols] = ei

        for t in range(m):
            rows = slice(t * nsub, (t + 1) * nsub)
            pr = apr_ref[t:t + 1, cols]
            pi = api_ref[t:t + 1, cols]
            sr[rows, :] = sr[rows, :] + (pr * ent_r - pi * ent_i)
            si[rows, :] = si[rows, :] + (pr * ent_i + pi * ent_r)
        y_ref[:, j * uw:(j + 1) * uw] = (
            jnp.dot(sr[...].astype(BF16), cre_ref[j], preferred_element_type=F32)
            - jnp.dot(si[...].astype(BF16), cim_ref[j], preferred_element_type=F32))

    y = _gelu_tanh(y_ref[...] + d_ref[...] * up.astype(F32))
    gate = jnp.dot(y.astype(BF16), wglu_ref[...], preferred_element_type=F32)
    out = y * jax.nn.sigmoid(gate)
    ms = jnp.mean(out * out, axis=-1, keepdims=True)
    outp = (out * lax.rsqrt(ms + EPS) * nw_ref[...]).astype(BF16)
    o_ref[...] = jnp.dot(permt_ref[...], outp, preferred_element_type=F32).astype(o_ref.dtype)


def _s5_discretise(lam_re, lam_im, log_dt, b_re, b_im):
    dt = jnp.exp(log_dt.astype(F32))[:, None]
    lr, li = lam_re.astype(F32), lam_im.astype(F32)
    mag = jnp.exp(lr * dt)
    ab_re = mag * jnp.cos(li * dt)
    ab_im = mag * jnp.sin(li * dt)
    den = lr * lr + li * li
    f_re = ((ab_re - 1.0) * lr + ab_im * li) / den
    f_im = (ab_im * lr - (ab_re - 1.0) * li) / den
    br, bi = b_re.astype(F32), b_im.astype(F32)
    bb_re = f_re[..., None] * br - f_im[..., None] * bi
    bb_im = f_re[..., None] * bi + f_im[..., None] * br
    return ab_re, ab_im, bb_re, bb_im


def _block_diag(blocks, per):
    g, r, c = blocks.shape
    b = blocks.reshape(g // per, per, r, c)
    eye = jnp.eye(per, dtype=blocks.dtype)
    return jnp.einsum("nirc,ij->nirjc", b, eye).reshape(g // per, per * r, per * c)


def s5_mixer(proj, bsz, seq, lam_re, lam_im, log_dt, b_re, b_im, c_re, c_im, d_skip, w_glu, norm_w, tc=512):
    per = S5_LANE_CHUNK // S5_STATE
    ab_re, ab_im, bb_re, bb_im = _s5_discretise(lam_re, lam_im, log_dt, b_re, b_im)
    bre = _block_diag(jnp.swapaxes(bb_re, 1, 2), per).astype(BF16)
    bim = _block_diag(jnp.swapaxes(bb_im, 1, 2), per).astype(BF16)
    cre = _block_diag(jnp.swapaxes(c_re.astype(F32), 1, 2), per).astype(BF16)
    cim = _block_diag(jnp.swapaxes(c_im.astype(F32), 1, 2), per).astype(BF16)
    nstate = ab_re.size
    tc = min(tc, seq)
    nct = seq // tc
    m = tc // SUBLANES
    steps = jnp.arange(1, m + 1, dtype=F32)[:, None, None]
    dt = jnp.exp(log_dt.astype(F32))[None, :, None]
    pow_mag = jnp.exp(steps * lam_re.astype(F32)[None] * dt)
    pow_ang = steps * lam_im.astype(F32)[None] * dt
    apow_re = (pow_mag * jnp.cos(pow_ang)).reshape(m, nstate)
    apow_im = (pow_mag * jnp.sin(pow_ang)).reshape(m, nstate)
    r = jnp.arange(tc)
    perm = (r[None, :] == ((r % SUBLANES) * m + r // SUBLANES)[:, None]).astype(BF16)
    uw = per * S5_CH_PER_GROUP
    const2 = lambda b, c: (0, 0)
    const3 = lambda b, c: (0, 0, 0)
    return pl.pallas_call(
        functools.partial(_s5_kernel, tc=tc),
        grid=(bsz, nct),
        in_specs=[
            pl.BlockSpec((tc, GROUP_WIDTH), lambda b, c: (b * nct + c, COL_S5)),
            pl.BlockSpec((tc, tc), const2),
            pl.BlockSpec((tc, tc), const2),
            pl.BlockSpec((S5_CHUNKS, uw, S5_LANE_CHUNK), const3),
            pl.BlockSpec((S5_CHUNKS, uw, S5_LANE_CHUNK), const3),
            pl.BlockSpec((S5_CHUNKS, S5_LANE_CHUNK, uw), const3),
            pl.BlockSpec((S5_CHUNKS, S5_LANE_CHUNK, uw), const3),
            pl.BlockSpec((1, nstate), const2),
            pl.BlockSpec((1, nstate), const2),
            pl.BlockSpec((m, nstate), const2),
            pl.BlockSpec((m, nstate), const2),
            pl.BlockSpec((1, GROUP_WIDTH), const2),
            pl.BlockSpec((GROUP_WIDTH, GROUP_WIDTH), const2),
            pl.BlockSpec((1, GROUP_WIDTH), const2),
        ],
        out_specs=pl.BlockSpec((tc, GROUP_WIDTH), lambda b, c: (b * nct + c, 0)),
        out_shape=jax.ShapeDtypeStruct((bsz * seq, GROUP_WIDTH), BF16),
        scratch_shapes=[
            pltpu.VMEM((2, tc, S5_LANE_CHUNK), F32),
            pltpu.VMEM((2, tc, S5_LANE_CHUNK), F32),
            pltpu.VMEM((tc, GROUP_WIDTH), F32),
            pltpu.VMEM((1, nstate), F32),
            pltpu.VMEM((1, nstate), F32),
        ],
        compiler_params=_cparams(("parallel", "arbitrary")),
        name="s5_mixer",
    )(proj, perm, perm.T, bre, bim, cre, cim, ab_re.reshape(1, nstate), ab_im.reshape(1, nstate),
      apow_re, apow_im, d_skip.reshape(1, GROUP_WIDTH).astype(F32), w_glu.astype(BF16), norm_w.reshape(1, GROUP_WIDTH).astype(F32))


def _rel_bucket(dist):
    n = jnp.maximum(dist, 0)
    max_exact = REL_BUCKETS // 2
    log_ratio = jnp.log(jnp.maximum(n, 1).astype(F32) / max_exact) / math.log(REL_MAX_DIST / max_exact)
    large = max_exact + (log_ratio * (REL_BUCKETS - max_exact)).astype(jnp.int32)
    large = jnp.minimum(large, REL_BUCKETS - 1)
    return jnp.where(n < max_exact, n, large)


def _bias_tiles(tbl, blk):
    assert blk >= REL_MAX_DIST
    i = jnp.arange(blk)[:, None]
    j = jnp.arange(blk)[None, :]
    buckets = jnp.stack([_rel_bucket(i - j), _rel_bucket(blk + i - j), _rel_bucket(jnp.full((blk, blk), 2 * blk))])
    onehot = (buckets[..., None] == jnp.arange(REL_BUCKETS)).astype(F32)
    tiles = jnp.einsum("tijk,hk->htij", onehot, tbl, precision=lax.Precision.HIGHEST) * LOG2E
    visible = jnp.stack([i >= j, jnp.ones((blk, blk), bool), jnp.ones((blk, blk), bool)])
    return jnp.where(visible[None], tiles, NEG_INF).astype(F32)


def _qk(q, k):
    return lax.dot_general(q, k, (((1,), (1,)), ((), ())), preferred_element_type=F32)


def _attn_kernel(qa_ref, qb_ref, k_ref, v_ref, bias_ref, p0_ref, p1_ref, o_hbm,
                 qs_ref, s_ref, mx_ref, acc_ref, kmean_ref, out_ref, stage_ref, sem,
                 *, mode, scale, heads, group, nq):
    blk = ATT_BLOCK
    bi = pl.program_id(0)
    i = pl.program_id(1)
    owns = (i, nq - 1 - i)
    q_refs = (qa_ref, qb_ref)
    lane = lax.broadcasted_iota(jnp.int32, (blk, LANES), 1)
    n_maps = 2 if mode == "diff" else 1
    nb = k_ref.shape[0] // blk
    reps = blk // LANES
    n_steps = nq + 1

    def rows(side):
        if isinstance(side, int):
            return slice(side * blk, (side + 1) * blk)
        return pl.ds(pl.multiple_of(side * blk, blk), blk)

    chunk = max(c for c in range(1, ATT_STEP_UNROLL + 1) if n_steps % c == 0)

    def run_steps(step):
        def body(o, carry):
            for c in range(chunk):
                step(o * chunk + c)
            return carry
        lax.fori_loop(0, n_steps // chunk, body, 0)

    def step_info(t):
        first = t <= i
        side = jnp.where(first, 0, 1)
        n = jnp.where(first, t, t - (i + 1))
        own = jnp.where(first, owns[0], owns[1])
        return side, n, jnp.minimum(own - n, 2)

    if mode == "moba":
        @pl.when(i == 0)
        def _():
            for h in range(heads):
                kf = k_ref[:, h * LANES:(h + 1) * LANES].astype(F32).reshape(nb, blk, LANES)
                kmean = jnp.sum(kf, axis=1) * (1.0 / blk)
                kmean_ref[h] = jnp.concatenate([kmean, jnp.zeros((GATE_ROWS - nb, LANES), F32)], axis=0)

    for g0 in range(0, heads, group):
        maps = [(h, mi) for h in range(g0, g0 + group) for mi in range(n_maps)]
        for side in range(2):
            own = owns[side]
            for h in range(g0, g0 + group):
                hs = slice(h * LANES, (h + 1) * LANES)
                q = q_refs[side][:, hs]
                if mode == "diff":
                    qf = q.astype(F32)
                    qs_ref[(h - g0) * 2, rows(side), :] = jnp.where(lane < DIFF_HEAD_DIM, qf, 0.0).astype(BF16)
                    qs_ref[(h - g0) * 2 + 1, rows(side), :] = jnp.where(lane >= DIFF_HEAD_DIM, qf, 0.0).astype(BF16)
                else:
                    hi, mid, lo = _split3(kmean_ref[h])
                    gate = _qk(hi, q) + _qk(mid, q) + _qk(lo, q)
                    sub = lax.broadcasted_iota(jnp.int32, (GATE_ROWS, blk), 0)
                    gate = jnp.where(sub < own, gate, NEG_INF)
                    penalty = jnp.zeros((GATE_ROWS, blk), F32)
                    for n in range(nb):
                        row = gate[n:n + 1, :]
                        beats = jnp.logical_and(
                            jnp.logical_or(gate > row, jnp.logical_and(gate == row, sub < n)), sub < nb)
                        rank = jnp.sum(beats.astype(F32), axis=0, keepdims=True)
                        keep = jnp.logical_or(jnp.logical_and(rank < MOBA_TOPK, n < own), n == own)
                        penalty = jnp.where(sub == n, jnp.where(keep, 0.0, NEG_INF), penalty)
                    pen_t = jnp.transpose(
                        jnp.concatenate([penalty, jnp.zeros((LANES - GATE_ROWS, blk), F32)], axis=0))
                    qs_ref[h - g0, rows(side), :] = jnp.concatenate([q, pen_t.astype(BF16)], axis=1)
            for idx in range(len(maps)):
                mx_ref[idx, rows(side), :] = jnp.full((blk, LANES), -jnp.inf, F32)

        def logits_step(t, maps=maps):
            side, n, tile_dist = step_info(t)
            start = pl.multiple_of(n * blk, blk)
            if mode == "moba":
                onehot = jnp.where(lane == n, 1.0, 0.0).astype(BF16)
            for idx, (h, mi) in enumerate(maps):
                hs = slice(h * LANES, (h + 1) * LANES)
                kn = k_ref[pl.ds(start, blk), hs]
                if mode == "moba":
                    kn = jnp.concatenate([kn, onehot], axis=1)
                s = _qk(qs_ref[idx, rows(side), :], kn) * (scale * LOG2E) + bias_ref[h, tile_dist]
                s_ref[idx, t] = s
                part = s[:, :LANES]
                for r in range(1, reps):
                    part = jnp.maximum(part, s[:, r * LANES:(r + 1) * LANES])
                mx_ref[idx, rows(side), :] = jnp.maximum(mx_ref[idx, rows(side), :], part)

        run_steps(logits_step)

        for side in range(2):
            for idx in range(len(maps)):
                mx_ref[idx, rows(side), :] = jnp.broadcast_to(
                    jnp.max(mx_ref[idx, rows(side), :], axis=-1, keepdims=True), (blk, LANES))
                acc_ref[idx, rows(side), :] = jnp.zeros((blk, 2 * LANES), F32)

        ones = jnp.ones((blk, LANES), BF16)

        def value_step(t, maps=maps):
            side, n, _ = step_info(t)
            start = pl.multiple_of(n * blk, blk)
            for idx, (h, mi) in enumerate(maps):
                hs = slice(h * LANES, (h + 1) * LANES)
                s = s_ref[idx, t]
                mb = mx_ref[idx, rows(side), :]
                p = jnp.concatenate([jnp.exp2(s[:, r * LANES:(r + 1) * LANES] - mb) for r in range(reps)],
                                    axis=1).astype(BF16)
                v_aug = jnp.concatenate([v_ref[pl.ds(start, blk), hs], ones], axis=1)
                acc_ref[idx, rows(side), :] += jnp.dot(p, v_aug, preferred_element_type=F32)

        run_steps(value_step)

        def normalised(idx, side):
            acc = acc_ref[idx, rows(side), :]
            return acc[:, :LANES] / acc[:, LANES:]

        for side in range(2):
            for h in range(g0, g0 + group):
                hs = slice(h * LANES, (h + 1) * LANES)
                i0 = (h - g0) * n_maps
                o = normalised(i0, side)
                if mode == "diff":
                    o = o - p0_ref[...] * normalised(i0 + 1, side)
                    ms = jnp.mean(o * o, axis=-1, keepdims=True)
                    stage_ref[side, :, hs] = (o * lax.rsqrt(ms + EPS) * p1_ref[...]).astype(stage_ref.dtype)
                else:
                    out_ref[side, :, hs] = o

    copies = []
    for side in range(2):
        if mode == "moba":
            o = out_ref[side]
            ms = jnp.mean(o * o, axis=-1, keepdims=True)
            stage_ref[side] = (o * lax.rsqrt(ms + EPS) * p1_ref[...]).astype(stage_ref.dtype)
        row0 = pl.multiple_of((bi * nq + owns[side]) * blk, blk)
        copies.append(pltpu.make_async_copy(stage_ref.at[side], o_hbm.at[pl.ds(row0, blk), :], sem.at[side]))
        copies[-1].start()
    for cp in copies:
        cp.wait()


def attention(proj, bsz, seq, col_q, col_k, col_v, bias, p0, p1, mode, scale, heads):
    blk = ATT_BLOCK
    nq = seq // blk
    width = heads * LANES
    assert width == GROUP_WIDTH and seq % blk == 0 and nq % 2 == 0
    nb = seq // blk
    assert nb <= GATE_ROWS
    n_maps = 2 if mode == "diff" else 1
    group = ATT_MAPS_PER_PASS // n_maps
    const2 = lambda b, i: (0, 0)
    once = pl.Buffered(1)
    return pl.pallas_call(
        functools.partial(_attn_kernel, mode=mode, scale=scale, heads=heads, group=group, nq=nq),
        grid=(bsz, nq // 2),
        in_specs=[
            pl.BlockSpec((blk, width), lambda b, i: (b * nq + i, col_q)),
            pl.BlockSpec((blk, width), lambda b, i: (b * nq + nq - 1 - i, col_q)),
            pl.BlockSpec((seq, width), lambda b, i: (b, col_k)),
            pl.BlockSpec((seq, width), lambda b, i: (b, col_v)),
            pl.BlockSpec((heads, 3, blk, blk), lambda b, i: (0, 0, 0, 0), pipeline_mode=once),
            pl.BlockSpec(p0.shape, const2, pipeline_mode=once),
            pl.BlockSpec(p1.shape, const2, pipeline_mode=once),
        ],
        out_specs=pl.BlockSpec(memory_space=pl.ANY),
        out_shape=jax.ShapeDtypeStruct((bsz * seq, width), BF16),
        scratch_shapes=[
            pltpu.VMEM((ATT_MAPS_PER_PASS, 2 * blk, LANES * (1 if mode == "diff" else 2)), BF16),
            pltpu.VMEM((ATT_MAPS_PER_PASS, nq + 1, blk, blk), F32),
            pltpu.VMEM((ATT_MAPS_PER_PASS, 2 * blk, LANES), F32),
            pltpu.VMEM((ATT_MAPS_PER_PASS, 2 * blk, 2 * LANES), F32),
            pltpu.VMEM((heads, GATE_ROWS, LANES), F32),
            pltpu.VMEM((2, blk, width), F32),
            pltpu.VMEM((2, blk, width), BF16),
            pltpu.SemaphoreType.DMA((2,)),
        ],
        compiler_params=_cparams(("parallel", "arbitrary")),
        name="attn_" + mode,
    )(proj, proj, proj, proj, bias, p0, p1)


def _softplus(x):
    return jnp.maximum(x, 0.0) + jnp.log(1.0 + jnp.exp(-jnp.abs(x)))


def _silu(x):
    return x * jax.nn.sigmoid(x)


def _ssd_kernel(z_ref, xbc_ref, dt_ref, dtt_ref, cw_ref, cb_ref, dtb_ref, dtbt_ref, a_ref, at_ref, dsk_ref,
                nw_ref, o_ref, ext_ref, st_ref, y_ref, *, lc):
    halo = SSD_HALO
    c = pl.program_id(1)

    @pl.when(c == 0)
    def _():
        ext_ref[0:halo, :] = jnp.zeros((halo, SSD_XBC), F32)
        st_ref[...] = jnp.zeros_like(st_ref)

    ext_ref[halo:halo + lc, :] = xbc_ref[...].astype(F32)
    conv = cb_ref[...] + cw_ref[SSD_CONV - 1:SSD_CONV, :] * ext_ref[halo:halo + lc, :]
    for kk in range(1, SSD_CONV):
        conv = conv + cw_ref[SSD_CONV - 1 - kk:SSD_CONV - kk, :] * ext_ref[halo - kk:halo - kk + lc, :]
    ext_ref[0:halo, :] = ext_ref[lc:lc + halo, :]
    xbc = _silu(conv)
    xs = xbc[:, :GROUP_WIDTH]

    dt_c = _softplus(dt_ref[...] + dtb_ref[...])
    dt_r = _softplus(dtt_ref[...] + dtbt_ref[...])
    a_c = dt_c * a_ref[...]
    a_r = dt_r * at_ref[...]
    row = lax.broadcasted_iota(jnp.int32, (lc, lc), 0)
    colm = lax.broadcasted_iota(jnp.int32, (lc, lc), 1)
    causal = row >= colm
    tri = causal.astype(BF16)
    tri_t = (colm >= row).astype(BF16)
    acum_c = _dot_exact_rhs(tri, a_c)
    acum_r = _dot_exact_lhs(a_r, tri_t)
    alast_c = acum_c[lc - 1:lc, :]

    lane = lax.broadcasted_iota(jnp.int32, (lc, LANES), 1)
    lo_half = lane < SSD_HEAD_DIM
    heads_per_group = SSD_HEADS // SSD_GROUPS
    for g in range(SSD_GROUPS):
        bg = xbc[:, GROUP_WIDTH + g * SSD_STATE:GROUP_WIDTH + (g + 1) * SSD_STATE].astype(BF16)
        cg = xbc[:, GROUP_WIDTH + SSD_BC + g * SSD_STATE:GROUP_WIDTH + SSD_BC + (g + 1) * SSD_STATE].astype(BF16)
        cb = _qk(cg, bg)
        for pr in range(heads_per_group // 2):
            ha = g * heads_per_group + 2 * pr
            hb = ha + 1
            tile = slice(ha * SSD_HEAD_DIM, (ha + 2) * SSD_HEAD_DIM)
            xp = xs[:, tile]

            def per_head(col_a, col_b):
                return jnp.where(lo_half, col_a, col_b)

            xdt = xp * per_head(dt_c[:, ha:ha + 1], dt_c[:, hb:hb + 1])
            ydiag = None
            for hh, keep in ((ha, lo_half), (hb, jnp.logical_not(lo_half))):
                seg = acum_c[:, hh:hh + 1] - acum_r[hh:hh + 1, :]
                decay = jnp.exp(jnp.where(causal, seg, NEG_INF))
                mm = (cb * decay).astype(BF16)
                part = jnp.dot(mm, jnp.where(keep, xdt, 0.0).astype(BF16), preferred_element_type=F32)
                ydiag = part if ydiag is None else ydiag + part
            st = st_ref[ha // 2]
            yoff = _qk(cg, st.astype(BF16)) * per_head(jnp.exp(acum_c[:, ha:ha + 1]), jnp.exp(acum_c[:, hb:hb + 1]))
            y_ref[:, tile] = ydiag + yoff + dsk_ref[:, tile] * xp
            to_end = per_head(jnp.exp(alast_c[:, ha:ha + 1] - acum_c[:, ha:ha + 1]),
                              jnp.exp(alast_c[:, hb:hb + 1] - acum_c[:, hb:hb + 1]))
            xdec_t = jnp.transpose(xdt * to_end).astype(BF16)
            sub = lax.broadcasted_iota(jnp.int32, (LANES, SSD_STATE), 0)
            chunk_decay = jnp.where(sub < SSD_HEAD_DIM, jnp.exp(alast_c[:, ha:ha + 1]),
                                    jnp.exp(alast_c[:, hb:hb + 1]))
            st_ref[ha // 2] = st * chunk_decay + jnp.dot(xdec_t, bg, preferred_element_type=F32)

    y = y_ref[...] * _silu(z_ref[...].astype(F32))
    gw = GROUP_WIDTH // SSD_GROUPS
    for g in range(SSD_GROUPS):
        yg = y[:, g * gw:(g + 1) * gw]
        ms = jnp.mean(yg * yg, axis=-1, keepdims=True)
        o_ref[:, g * gw:(g + 1) * gw] = (yg * lax.rsqrt(ms + EPS) * nw_ref[:, g * gw:(g + 1) * gw]).astype(o_ref.dtype)


def _pad_lanes(v):
    return jnp.pad(v.astype(F32), (0, LANES - v.shape[0])).reshape(1, LANES)


def ssd_mixer(proj, dt_raw, bsz, seq, conv_w, conv_b, dt_bias, a_log, d_skip, norm_w, lc=128):
    nc = seq // lc
    t = bsz * seq
    a = -jnp.exp(a_log.astype(F32))
    dt_t = jnp.transpose(dt_raw[:, :SSD_HEADS])
    col16 = lambda v: jnp.broadcast_to(v.astype(F32)[:, None], (SSD_HEADS, LANES))
    dskip = jnp.repeat(d_skip.astype(F32), SSD_HEAD_DIM).reshape(1, GROUP_WIDTH)
    const2 = lambda b, c: (0, 0)
    return pl.pallas_call(
        functools.partial(_ssd_kernel, lc=lc),
        grid=(bsz, nc),
        in_specs=[
            pl.BlockSpec((lc, GROUP_WIDTH), lambda b, c: (b * nc + c, COL_Z)),
            pl.BlockSpec((lc, SSD_XBC), lambda b, c: (b * nc + c, COL_XBC)),
            pl.BlockSpec((lc, LANES), lambda b, c: (b * nc + c, 0)),
            pl.BlockSpec((SSD_HEADS, lc), lambda b, c: (0, b * nc + c)),
            pl.BlockSpec((SSD_CONV, SSD_XBC), const2),
            pl.BlockSpec((1, SSD_XBC), const2),
            pl.BlockSpec((1, LANES), const2),
            pl.BlockSpec((SSD_HEADS, LANES), const2),
            pl.BlockSpec((1, LANES), const2),
            pl.BlockSpec((SSD_HEADS, LANES), const2),
            pl.BlockSpec((1, GROUP_WIDTH), const2),
            pl.BlockSpec((1, GROUP_WIDTH), const2),
        ],
        out_specs=pl.BlockSpec((lc, GROUP_WIDTH), lambda b, c: (b * nc + c, 0)),
        out_shape=jax.ShapeDtypeStruct((t, GROUP_WIDTH), BF16),
        scratch_shapes=[
            pltpu.VMEM((lc + SSD_HALO, SSD_XBC), F32),
            pltpu.VMEM((SSD_HEADS // 2, 2 * SSD_HEAD_DIM, SSD_STATE), F32),
            pltpu.VMEM((lc, GROUP_WIDTH), F32),
        ],
        compiler_params=_cparams(("parallel", "arbitrary")),
        name="ssd_mixer",
    )(proj, proj, dt_raw, dt_t, conv_w.reshape(SSD_CONV, SSD_XBC).astype(F32),
      conv_b.reshape(1, SSD_XBC).astype(F32), _pad_lanes(dt_bias), col16(dt_bias), _pad_lanes(a), col16(a),
      dskip, norm_w.reshape(1, GROUP_WIDTH).astype(F32))


def kernel(x, rel_bias_table, attn_norm_w, w_in, s5_lam_re, s5_lam_im, s5_log_dt, s5_b_re, s5_b_im, s5_c_re, s5_c_im, s5_d, s5_w_glu, s5_out_norm_w, diff_lam_q1, diff_lam_k1, diff_lam_q2, diff_lam_k2, diff_subln_w, moba_out_norm_w, ssd_conv_w, ssd_conv_b, ssd_dt_bias, ssd_a_log, ssd_d, ssd_norm_w, w_out, mlp_norm_w, w_up, w_down, final_norm_w):
    bsz, seq, d_model = x.shape
    depth = w_in.shape[0]
    t = bsz * seq
    x = x.reshape(t, d_model).astype(F32)

    w_in_t = jnp.swapaxes(w_in, 1, 2)
    w_dt_b = jnp.pad(w_in[:, :, PROJ_MAIN:], ((0, 0), (0, 0), (0, LANES - SSD_HEADS))).astype(BF16)

    tbl = rel_bias_table.astype(F32).T
    bias_diff = _bias_tiles(tbl[:DIFF_HEADS], ATT_BLOCK)
    bias_moba = _bias_tiles(tbl[DIFF_HEADS:], ATT_BLOCK)

    def row_scale(ssq):
        r = lax.rsqrt(jnp.sum(ssq, axis=(0, 2)) * (1.0 / d_model) + EPS)
        return jnp.broadcast_to(r[:, None], (t, LANES))

    h, scale = rmsnorm(x, attn_norm_w[0], BF16), None
    for l in range(depth):
        proj = matmul_wstat([h], w_in_t, l, PROJ_MAIN, BF16, row_scale=scale, w_transposed=True)
        dt_raw = matmul(h, w_dt_b, l, LANES, F32, row_scale=scale)

        y_s5 = s5_mixer(proj, bsz, seq, s5_lam_re[l], s5_lam_im[l], s5_log_dt[l], s5_b_re[l], s5_b_im[l],
                        s5_c_re[l], s5_c_im[l], s5_d[l], s5_w_glu[l], s5_out_norm_w[l])

        lam_init = 0.8 - 0.6 * math.exp(-0.3 * l)
        lam = (jnp.exp(jnp.sum(diff_lam_q1[l].astype(F32) * diff_lam_k1[l].astype(F32)))
               - jnp.exp(jnp.sum(diff_lam_q2[l].astype(F32) * diff_lam_k2[l].astype(F32))) + lam_init)
        y_diff = attention(proj, bsz, seq, COL_DQ, COL_DK, COL_DV, bias_diff,
                           jnp.broadcast_to(lam, (1, LANES)).astype(F32),
                           (diff_subln_w[l].astype(F32) * (1.0 - lam_init)).reshape(1, LANES),
                           "diff", DIFF_HEAD_DIM ** -0.5, DIFF_HEADS)
        y_moba = attention(proj, bsz, seq, COL_MQ, COL_MK, COL_MV, bias_moba,
                           jnp.zeros((1, LANES), F32),
                           moba_out_norm_w[l].astype(F32).reshape(1, GROUP_WIDTH),
                           "moba", (GROUP_WIDTH // MOBA_HEADS) ** -0.5, MOBA_HEADS)
        y_ssd = ssd_mixer(proj, dt_raw, bsz, seq, ssd_conv_w[l], ssd_conv_b[l], ssd_dt_bias[l], ssd_a_log[l],
                          ssd_d[l], ssd_norm_w[l])

        x, h, ssq = matmul_wstat([y_s5, y_diff, y_moba, y_ssd], w_out, l, d_model, F32, residual=x,
                                 norm_w=mlp_norm_w[l])
        u, w_down_b = matmul_wstat([h], w_up, l, w_up.shape[2], BF16, act="relu2", row_scale=row_scale(ssq),
                                   cast_through=w_down)
        if l + 1 < depth:
            x, h, ssq = matmul(u, w_down_b[None], 0, d_model, F32, residual=x, norm_w=attn_norm_w[l + 1])
            scale = row_scale(ssq)
        else:
            x = matmul(u, w_down_b[None], 0, d_model, F32, residual=x)

    return rmsnorm(x, final_norm_w, F32).reshape(bsz, seq, d_model)
```

```python
import functools
import math

import jax
import jax.numpy as jnp
from jax import lax
from jax.experimental import pallas as pl
from jax.experimental.pallas import tpu as pltpu

F32 = jnp.float32
BF16 = jnp.bfloat16
EPS = 1e-6
NEG_INF = -1e30

GROUP_WIDTH = 1024
S5_CH_PER_GROUP = 16
S5_STATE = 64
DIFF_HEADS = 8
DIFF_HEAD_DIM = 64
MOBA_HEADS = 8
MOBA_BLOCK = 256
MOBA_TOPK = 3
SSD_HEAD_DIM = 64
SSD_HEADS = 16
SSD_GROUPS = 4
SSD_STATE = 128
SSD_CONV = 4
SSD_HALO = 8
SSD_BC = SSD_GROUPS * SSD_STATE
SSD_XBC = GROUP_WIDTH + 2 * SSD_BC
REL_BUCKETS = 32
REL_MAX_DIST = 128

COL_S5 = 0
COL_DQ, COL_DK, COL_DV = 1, 2, 3
COL_MQ, COL_MK, COL_MV = 4, 5, 6
COL_Z = 7
COL_XBC = 4
PROJ_MAIN = 10 * GROUP_WIDTH

LANES = 128
SUBLANES = 8
ATT_BLOCK = 256
ATT_MAPS_PER_PASS = 8
ATT_STEP_UNROLL = 3
GATE_ROWS = 16
LOG2E = math.log2(math.e)
S5_LANE_CHUNK = 512
S5_CHUNKS = (GROUP_WIDTH // S5_CH_PER_GROUP) * S5_STATE // S5_LANE_CHUNK
WSTAT_PIECES = 4
VMEM_LIMIT = 56 * 1024 * 1024


def _cparams(sem):
    return pltpu.CompilerParams(dimension_semantics=sem, vmem_limit_bytes=VMEM_LIMIT)


def _rmsnorm_kernel(x_ref, w_ref, o_ref):
    x = x_ref[...].astype(F32)
    ms = jnp.mean(x * x, axis=-1, keepdims=True)
    o_ref[...] = (x * lax.rsqrt(ms + EPS) * w_ref[...]).astype(o_ref.dtype)


def rmsnorm(x, w, out_dtype, tm=512):
    t, d = x.shape
    return pl.pallas_call(
        _rmsnorm_kernel,
        grid=(t // tm,),
        in_specs=[pl.BlockSpec((tm, d), lambda i: (i, 0)), pl.BlockSpec((1, d), lambda i: (0, 0))],
        out_specs=pl.BlockSpec((tm, d), lambda i: (i, 0)),
        out_shape=jax.ShapeDtypeStruct((t, d), out_dtype),
        compiler_params=_cparams(("parallel",)),
        name="rmsnorm",
    )(x, w.reshape(1, d).astype(F32))


def _split_refs(refs, n_lhs, n_w, has_res, has_scale, has_norm, has_cast):
    it = iter(refs)
    take = lambda n: [next(it) for _ in range(n)]
    r = {"lhs": take(n_lhs), "w": take(n_w)}
    r["res"] = next(it) if has_res else None
    r["scale"] = next(it) if has_scale else None
    r["normw"] = next(it) if has_norm else None
    r["cast_in"] = next(it) if has_cast else None
    r["out"] = next(it)
    r["xw"], r["ssq"] = (next(it), next(it)) if has_norm else (None, None)
    r["cast_out"] = next(it) if has_cast else None
    r["scratch"] = list(it)
    return r


def _mm_epilogue(acc, r, act):
    reps = acc.shape[1] // LANES
    if r["scale"] is not None:
        acc = acc * jnp.concatenate([r["scale"][...]] * reps, axis=1)
    if act == "relu2":
        p = jnp.maximum(acc, 0.0)
        acc = p * p
    if r["res"] is not None:
        acc = r["res"][...] + acc
    r["out"][...] = acc.astype(r["out"].dtype)
    if r["normw"] is not None:
        r["xw"][...] = (acc * r["normw"][...]).astype(BF16)
        sq = acc * acc
        part = sq[:, :LANES]
        for c in range(1, reps):
            part = part + sq[:, c * LANES:(c + 1) * LANES]
        r["ssq"][...] = part


def _mm_kernel(*refs, nk, act, flags):
    r = _split_refs(refs, 1, 1, *flags)
    a_ref, w_ref, acc_ref = r["lhs"][0], r["w"][0], r["scratch"][0]
    k = pl.program_id(2)

    def partial_dot():
        return jnp.dot(a_ref[...], w_ref[...], preferred_element_type=F32)

    if nk == 1:
        _mm_epilogue(partial_dot(), r, act)
        return

    @pl.when(k == 0)
    def _():
        acc_ref[...] = partial_dot()

    @pl.when(jnp.logical_and(k > 0, k < nk - 1))
    def _():
        acc_ref[...] += partial_dot()

    @pl.when(k == nk - 1)
    def _():
        _mm_epilogue(acc_ref[...] + partial_dot(), r, act)


def _norm_out(m, n_out, tm, tn, nj, idx, idx3):
    specs = [pl.BlockSpec((tm, tn), idx), pl.BlockSpec((None, tm, LANES), idx3)]
    shapes = [jax.ShapeDtypeStruct((m, n_out), BF16), jax.ShapeDtypeStruct((nj, m, LANES), F32)]
    return specs, shapes


def matmul(lhs, w3, layer, n_out, out_dtype, act=None, residual=None, row_scale=None, norm_w=None,
           tm=1024, tn=1024, tk=2048):
    m, kdim = lhs.shape[0], w3.shape[1]
    tm, tn, tk = min(tm, m), min(tn, n_out), min(tk, kdim)
    nk = kdim // tk
    assert m % tm == 0 and n_out % tn == 0 and kdim % tk == 0
    tile = lambda i, j, k: (i, j)
    in_specs = [pl.BlockSpec((tm, tk), lambda i, j, k: (i, k)),
                pl.BlockSpec((None, tk, tn), lambda i, j, k: (layer, k, j))]
    args = [lhs, w3]
    if residual is not None:
        in_specs.append(pl.BlockSpec((tm, tn), tile))
        args.append(residual)
    if row_scale is not None:
        in_specs.append(pl.BlockSpec((tm, LANES), lambda i, j, k: (i, 0)))
        args.append(row_scale)
    out_specs = [pl.BlockSpec((tm, tn), tile)]
    out_shape = [jax.ShapeDtypeStruct((m, n_out), out_dtype)]
    if norm_w is not None:
        in_specs.append(pl.BlockSpec((1, tn), lambda i, j, k: (0, j)))
        args.append(norm_w.reshape(1, n_out).astype(F32))
        specs, shapes = _norm_out(m, n_out, tm, tn, n_out // tn, tile, lambda i, j, k: (j, i, 0))
        out_specs += specs
        out_shape += shapes
    flags = (residual is not None, row_scale is not None, norm_w is not None, False)
    res = pl.pallas_call(
        functools.partial(_mm_kernel, nk=nk, act=act, flags=flags),
        grid=(m // tm, n_out // tn, nk),
        in_specs=in_specs,
        out_specs=out_specs,
        out_shape=out_shape,
        scratch_shapes=[pltpu.VMEM((tm, tn) if nk > 1 else (8, LANES), F32)],
        compiler_params=_cparams(("parallel", "parallel", "arbitrary")),
        name="matmul_" + (act or "lin") + ("_res" if residual is not None else ""),
    )(*args)
    return res if norm_w is not None else res[0]


def _mm_wstat_kernel(*refs, act, n_lhs, flags, w_transposed):
    r = _split_refs(refs, n_lhs, WSTAT_PIECES, *flags)
    wb_ref = r["scratch"][0]
    kp = r["w"][0].shape[1 if w_transposed else 0]

    @pl.when(pl.program_id(1) == 0)
    def _():
        for p, w_ref in enumerate(r["w"]):
            if w_transposed:
                wb_ref[:, p * kp:(p + 1) * kp] = w_ref[...].astype(BF16)
            else:
                wb_ref[p * kp:(p + 1) * kp, :] = w_ref[...].astype(BF16)

    if r["cast_in"] is not None:
        r["cast_out"][...] = r["cast_in"][...].astype(BF16)

    kw = r["lhs"][0].shape[1]
    acc = None
    for i, a_ref in enumerate(r["lhs"]):
        if w_transposed:
            p = lax.dot_general(a_ref[...], wb_ref[:, i * kw:(i + 1) * kw], (((1,), (1,)), ((), ())),
                                preferred_element_type=F32)
        else:
            p = jnp.dot(a_ref[...], wb_ref[i * kw:(i + 1) * kw, :], preferred_element_type=F32)
        acc = p if acc is None else acc + p
    _mm_epilogue(acc, r, act)


def matmul_wstat(lhs, w3, layer, n_out, out_dtype, act=None, residual=None, row_scale=None, norm_w=None,
                 w_transposed=False, cast_through=None, tm=1024, tn=512):
    lhs = list(lhs)
    m = lhs[0].shape[0]
    kdim = w3.shape[2 if w_transposed else 1]
    assert sum(a.shape[1] for a in lhs) == kdim and all(a.shape[1] == lhs[0].shape[1] for a in lhs)
    assert m % tm == 0 and n_out % tn == 0 and kdim % WSTAT_PIECES == 0
    ni, nj, kp = m // tm, n_out // tn, kdim // WSTAT_PIECES
    assert ni > WSTAT_PIECES
    tile = lambda j, i: (i, j)
    in_specs = [pl.BlockSpec((tm, a.shape[1]), lambda j, i: (i, 0)) for a in lhs]
    for p in range(WSTAT_PIECES):
        col = lambda j, i, p=p: jnp.minimum(j + (i > p).astype(jnp.int32), nj - 1)
        if w_transposed:
            in_specs.append(pl.BlockSpec((None, tn, kp), lambda j, i, p=p, col=col: (layer, col(j, i), p)))
        else:
            in_specs.append(pl.BlockSpec((None, kp, tn), lambda j, i, p=p, col=col: (layer, p, col(j, i))))
    args = lhs + [w3] * WSTAT_PIECES
    if residual is not None:
        in_specs.append(pl.BlockSpec((tm, tn), tile))
        args.append(residual)
    if row_scale is not None:
        in_specs.append(pl.BlockSpec((tm, LANES), lambda j, i: (i, 0)))
        args.append(row_scale)
    if norm_w is not None:
        in_specs.append(pl.BlockSpec((1, tn), lambda j, i: (0, j)))
        args.append(norm_w.reshape(1, n_out).astype(F32))
    out_specs = [pl.BlockSpec((tm, tn), tile)]
    out_shape = [jax.ShapeDtypeStruct((m, n_out), out_dtype)]
    if norm_w is not None:
        specs, shapes = _norm_out(m, n_out, tm, tn, nj, tile, lambda j, i: (j, i, 0))
        out_specs += specs
        out_shape += shapes
    if cast_through is not None:
        steps = nj * ni
        rows, cols = cast_through.shape[1:]
        assert rows % steps == 0
        slab = rows // steps
        in_specs.append(pl.BlockSpec((None, slab, cols), lambda j, i: (layer, j * ni + i, 0)))
        args.append(cast_through)
        out_specs.append(pl.BlockSpec((slab, cols), lambda j, i: (j * ni + i, 0)))
        out_shape.append(jax.ShapeDtypeStruct((rows, cols), BF16))
    flags = (residual is not None, row_scale is not None, norm_w is not None, cast_through is not None)
    res = pl.pallas_call(
        functools.partial(_mm_wstat_kernel, act=act, n_lhs=len(lhs), flags=flags, w_transposed=w_transposed),
        grid=(nj, ni),
        in_specs=in_specs,
        out_specs=out_specs,
        out_shape=out_shape,
        scratch_shapes=[pltpu.VMEM((tn, kdim) if w_transposed else (kdim, tn), BF16)],
        compiler_params=_cparams(("arbitrary", "arbitrary")),
        name="matmul_wstat_" + (act or "lin") + ("_res" if residual is not None else ""),
    )(*args)
    return res if len(res) > 1 else res[0]


def _split3(x):
    hi = x.astype(BF16)
    r1 = x - hi.astype(F32)
    mid = r1.astype(BF16)
    lo = (r1 - mid.astype(F32)).astype(BF16)
    return hi, mid, lo


def _dot_exact_lhs(x, sel):
    hi, mid, lo = _split3(x)
    d = functools.partial(jnp.dot, preferred_element_type=F32)
    return d(hi, sel) + d(mid, sel) + d(lo, sel)


def _dot_exact_rhs(sel, x):
    hi, mid, lo = _split3(x)
    d = functools.partial(jnp.dot, preferred_element_type=F32)
    return d(sel, hi) + d(sel, mid) + d(sel, lo)


def _gelu_tanh(x):
    c = math.sqrt(2.0 / math.pi)
    return 0.5 * x * (1.0 + jnp.tanh(c * (x + 0.044715 * (x * x * x))))


def _s5_kernel(u_ref, perm_ref, permt_ref, bre_ref, bim_ref, cre_ref, cim_ref, are_ref, aim_ref,
               apr_ref, api_ref, d_ref, wglu_ref, nw_ref, o_ref, sr_ref, si_ref, y_ref, st_re, st_im, *, tc):
    nsub = SUBLANES
    m = tc // nsub

    @pl.when(pl.program_id(1) == 0)
    def _():
        st_re[...] = jnp.zeros_like(st_re)
        st_im[...] = jnp.zeros_like(st_im)

    up = jnp.dot(perm_ref[...], u_ref[...], preferred_element_type=F32).astype(BF16)

    cw = S5_LANE_CHUNK
    uw = cw // S5_STATE * S5_CH_PER_GROUP
    sub = lax.broadcasted_iota(jnp.int32, (nsub, cw), 0)
    for j in range(S5_CHUNKS):
        cols = slice(j * cw, (j + 1) * cw)
        sr, si = sr_ref.at[j % 2], si_ref.at[j % 2]
        uj = up[:, j * uw:(j + 1) * uw]
        sr[...] = jnp.dot(uj, bre_ref[j], preferred_element_type=F32)
        si[...] = jnp.dot(uj, bim_ref[j], preferred_element_type=F32)
        ar = jnp.broadcast_to(are_ref[:, cols], (nsub, cw))
        ai = jnp.broadcast_to(aim_ref[:, cols], (nsub, cw))

        loc_r = loc_i = jnp.zeros((nsub, cw), F32)
        for t in range(m):
            rows = slice(t * nsub, (t + 1) * nsub)
            loc_r, loc_i = (ar * loc_r - ai * loc_i + sr[rows, :], ar * loc_i + ai * loc_r + si[rows, :])
            sr[rows, :] = loc_r
            si[rows, :] = loc_i

        er, ei = st_re[:, cols], st_im[:, cols]
        amr, ami = apr_ref[(m - 1) * nsub:(m - 1) * nsub + 1, cols], api_ref[(m - 1) * nsub:(m - 1) * nsub + 1, cols]
        ent_r = ent_i = jnp.zeros((nsub, cw), F32)
        for q in range(nsub):
            ent_r = jnp.where(sub == q, er, ent_r)
            ent_i = jnp.where(sub == q, ei, ent_i)
            er, ei = (amr * er - ami * ei + loc_r[q:q + 1, :], amr * ei + ami * er + loc_i[q:q + 1, :])
        st_re[:, cols] = er
        st_im[:, cols] = ei

        for t in range(m):
            rows = slice(t * nsub, (t + 1) * nsub)
            pr = apr_ref[t * nsub:(t + 1) * nsub, cols]
            pi = api_ref[t * nsub:(t + 1) * nsub, cols]
            sr[rows, :] = sr[rows, :] + (pr * ent_r - pi * ent_i)
            si[rows, :] = si[rows, :] + (pr * ent_i + pi * ent_r)
        y_ref[:, j * uw:(j + 1) * uw] = (
            jnp.dot(sr[...].astype(BF16), cre_ref[j], preferred_element_type=F32)
            - jnp.dot(si[...].astype(BF16), cim_ref[j], preferred_element_type=F32))

    y = _gelu_tanh(y_ref[...] + d_ref[...] * up.astype(F32))
    gate = jnp.dot(y.astype(BF16), wglu_ref[...], preferred_element_type=F32)
    out = y * jax.nn.sigmoid(gate)
    ms = jnp.mean(out * out, axis=-1, keepdims=True)
    outp = (out * lax.rsqrt(ms + EPS) * nw_ref[...]).astype(BF16)
    o_ref[...] = jnp.dot(permt_ref[...], outp, preferred_element_type=F32).astype(o_ref.dtype)


def _s5_discretise(lam_re, lam_im, log_dt, b_re, b_im):
    dt = jnp.exp(log_dt.astype(F32))[:, None]
    lr, li = lam_re.astype(F32), lam_im.astype(F32)
    mag = jnp.exp(lr * dt)
    ab_re = mag * jnp.cos(li * dt)
    ab_im = mag * jnp.sin(li * dt)
    den = lr * lr + li * li
    f_re = ((ab_re - 1.0) * lr + ab_im * li) / den
    f_im = (ab_im * lr - (ab_re - 1.0) * li) / den
    br, bi = b_re.astype(F32), b_im.astype(F32)
    bb_re = f_re[..., None] * br - f_im[..., None] * bi
    bb_im = f_re[..., None] * bi + f_im[..., None] * br
    return ab_re, ab_im, bb_re, bb_im


def _block_diag(blocks, per):
    g, r, c = blocks.shape
    b = blocks.reshape(g // per, per, r, c)
    eye = jnp.eye(per, dtype=blocks.dtype)
    return jnp.einsum("nirc,ij->nirjc", b, eye).reshape(g // per, per * r, per * c)


def s5_mixer(proj, bsz, seq, lam_re, lam_im, log_dt, b_re, b_im, c_re, c_im, d_skip, w_glu, norm_w, tc=512):
    per = S5_LANE_CHUNK // S5_STATE
    ab_re, ab_im, bb_re, bb_im = _s5_discretise(lam_re, lam_im, log_dt, b_re, b_im)
    bre = _block_diag(jnp.swapaxes(bb_re, 1, 2), per).astype(BF16)
    bim = _block_diag(jnp.swapaxes(bb_im, 1, 2), per).astype(BF16)
    cre = _block_diag(jnp.swapaxes(c_re.astype(F32), 1, 2), per).astype(BF16)
    cim = _block_diag(jnp.swapaxes(c_im.astype(F32), 1, 2), per).astype(BF16)
    nstate = ab_re.size
    tc = min(tc, seq)
    nct = seq // tc
    m = tc // SUBLANES
    steps = jnp.arange(1, m + 1, dtype=F32)[:, None, None]
    dt = jnp.exp(log_dt.astype(F32))[None, :, None]
    pow_mag = jnp.exp(steps * lam_re.astype(F32)[None] * dt)
    pow_ang = steps * lam_im.astype(F32)[None] * dt
    apow_re = jnp.repeat((pow_mag * jnp.cos(pow_ang)).reshape(m, nstate), SUBLANES, axis=0)
    apow_im = jnp.repeat((pow_mag * jnp.sin(pow_ang)).reshape(m, nstate), SUBLANES, axis=0)
    r = jnp.arange(tc)
    perm = (r[None, :] == ((r % SUBLANES) * m + r // SUBLANES)[:, None]).astype(BF16)
    uw = per * S5_CH_PER_GROUP
    const2 = lambda b, c: (0, 0)
    const3 = lambda b, c: (0, 0, 0)
    return pl.pallas_call(
        functools.partial(_s5_kernel, tc=tc),
        grid=(bsz, nct),
        in_specs=[
            pl.BlockSpec((tc, GROUP_WIDTH), lambda b, c: (b * nct + c, COL_S5)),
            pl.BlockSpec((tc, tc), const2),
            pl.BlockSpec((tc, tc), const2),
            pl.BlockSpec((S5_CHUNKS, uw, S5_LANE_CHUNK), const3),
            pl.BlockSpec((S5_CHUNKS, uw, S5_LANE_CHUNK), const3),
            pl.BlockSpec((S5_CHUNKS, S5_LANE_CHUNK, uw), const3),
            pl.BlockSpec((S5_CHUNKS, S5_LANE_CHUNK, uw), const3),
            pl.BlockSpec((1, nstate), const2),
            pl.BlockSpec((1, nstate), const2),
            pl.BlockSpec((m * SUBLANES, nstate), const2, pipeline_mode=pl.Buffered(1)),
            pl.BlockSpec((m * SUBLANES, nstate), const2, pipeline_mode=pl.Buffered(1)),
            pl.BlockSpec((1, GROUP_WIDTH), const2),
            pl.BlockSpec((GROUP_WIDTH, GROUP_WIDTH), const2),
            pl.BlockSpec((1, GROUP_WIDTH), const2),
        ],
        out_specs=pl.BlockSpec((tc, GROUP_WIDTH), lambda b, c: (b * nct + c, 0)),
        out_shape=jax.ShapeDtypeStruct((bsz * seq, GROUP_WIDTH), BF16),
        scratch_shapes=[
            pltpu.VMEM((2, tc, S5_LANE_CHUNK), F32),
            pltpu.VMEM((2, tc, S5_LANE_CHUNK), F32),
            pltpu.VMEM((tc, GROUP_WIDTH), F32),
            pltpu.VMEM((1, nstate), F32),
            pltpu.VMEM((1, nstate), F32),
        ],
        compiler_params=_cparams(("parallel", "arbitrary")),
        name="s5_mixer",
    )(proj, perm, perm.T, bre, bim, cre, cim, ab_re.reshape(1, nstate), ab_im.reshape(1, nstate),
      apow_re, apow_im, d_skip.reshape(1, GROUP_WIDTH).astype(F32), w_glu.astype(BF16), norm_w.reshape(1, GROUP_WIDTH).astype(F32))


def _rel_bucket(dist):
    n = jnp.maximum(dist, 0)
    max_exact = REL_BUCKETS // 2
    log_ratio = jnp.log(jnp.maximum(n, 1).astype(F32) / max_exact) / math.log(REL_MAX_DIST / max_exact)
    large = max_exact + (log_ratio * (REL_BUCKETS - max_exact)).astype(jnp.int32)
    large = jnp.minimum(large, REL_BUCKETS - 1)
    return jnp.where(n < max_exact, n, large)


def _bias_tiles(tbl, blk):
    assert blk >= REL_MAX_DIST
    i = jnp.arange(blk)[:, None]
    j = jnp.arange(blk)[None, :]
    buckets = jnp.stack([_rel_bucket(i - j), _rel_bucket(blk + i - j), _rel_bucket(jnp.full((blk, blk), 2 * blk))])
    onehot = (buckets[..., None] == jnp.arange(REL_BUCKETS)).astype(F32)
    tiles = jnp.einsum("tijk,hk->htij", onehot, tbl, precision=lax.Precision.HIGHEST) * LOG2E
    visible = jnp.stack([i >= j, jnp.ones((blk, blk), bool), jnp.ones((blk, blk), bool)])
    return jnp.where(visible[None], tiles, NEG_INF).astype(F32)


def _qk(q, k):
    return lax.dot_general(q, k, (((1,), (1,)), ((), ())), preferred_element_type=F32)


def _attn_kernel(qa_ref, qb_ref, k_ref, v_ref, bias_ref, p0_ref, p1_ref, o_hbm,
                 qs_ref, s_ref, mx_ref, acc_ref, kmean_ref, out_ref, stage_ref, sem,
                 *, mode, scale, heads, group, nq):
    blk = ATT_BLOCK
    bi = pl.program_id(0)
    i = pl.program_id(1)
    owns = (i, nq - 1 - i)
    q_refs = (qa_ref, qb_ref)
    lane = lax.broadcasted_iota(jnp.int32, (blk, LANES), 1)
    n_maps = 2 if mode == "diff" else 1
    nb = k_ref.shape[0] // blk
    reps = blk // LANES
    n_steps = nq + 1

    def rows(side):
        if isinstance(side, int):
            return slice(side * blk, (side + 1) * blk)
        return pl.ds(pl.multiple_of(side * blk, blk), blk)

    chunk = max(c for c in range(1, ATT_STEP_UNROLL + 1) if n_steps % c == 0)

    def run_steps(step):
        def body(o, carry):
            for c in range(chunk):
                step(o * chunk + c)
            return carry
        lax.fori_loop(0, n_steps // chunk, body, 0)

    def step_info(t):
        first = t <= i
        side = jnp.where(first, 0, 1)
        n = jnp.where(first, t, t - (i + 1))
        own = jnp.where(first, owns[0], owns[1])
        return side, n, jnp.minimum(own - n, 2)

    if mode == "moba":
        @pl.when(i == 0)
        def _():
            for h in range(heads):
                kf = k_ref[:, h * LANES:(h + 1) * LANES].astype(F32).reshape(nb, blk, LANES)
                kmean = jnp.sum(kf, axis=1) * (1.0 / blk)
                kmean_ref[h] = jnp.concatenate([kmean, jnp.zeros((GATE_ROWS - nb, LANES), F32)], axis=0)

    for g0 in range(0, heads, group):
        maps = [(h, mi) for h in range(g0, g0 + group) for mi in range(n_maps)]
        for side in range(2):
            own = owns[side]
            for h in range(g0, g0 + group):
                hs = slice(h * LANES, (h + 1) * LANES)
                q = q_refs[side][:, hs]
                if mode == "diff":
                    qf = q.astype(F32)
                    qs_ref[(h - g0) * 2, rows(side), :] = jnp.where(lane < DIFF_HEAD_DIM, qf, 0.0).astype(BF16)
                    qs_ref[(h - g0) * 2 + 1, rows(side), :] = jnp.where(lane >= DIFF_HEAD_DIM, qf, 0.0).astype(BF16)
                else:
                    hi, mid, lo = _split3(kmean_ref[h])
                    gate = _qk(hi, q) + _qk(mid, q) + _qk(lo, q)
                    sub = lax.broadcasted_iota(jnp.int32, (GATE_ROWS, blk), 0)
                    gate = jnp.where(sub < own, gate, NEG_INF)
                    penalty = jnp.zeros((GATE_ROWS, blk), F32)
                    for n in range(nb):
                        row = gate[n:n + 1, :]
                        beats = jnp.logical_and(
                            jnp.logical_or(gate > row, jnp.logical_and(gate == row, sub < n)), sub < nb)
                        rank = jnp.sum(beats.astype(F32), axis=0, keepdims=True)
                        keep = jnp.logical_or(jnp.logical_and(rank < MOBA_TOPK, n < own), n == own)
                        penalty = jnp.where(sub == n, jnp.where(keep, 0.0, NEG_INF), penalty)
                    pen_t = jnp.transpose(
                        jnp.concatenate([penalty, jnp.zeros((LANES - GATE_ROWS, blk), F32)], axis=0))
                    qs_ref[h - g0, rows(side), :] = jnp.concatenate([q, pen_t.astype(BF16)], axis=1)
            for idx in range(len(maps)):
                mx_ref[idx, rows(side), :] = jnp.full((blk, LANES), -jnp.inf, F32)

        def logits_step(t, maps=maps):
            side, n, tile_dist = step_info(t)
            start = pl.multiple_of(n * blk, blk)
            if mode == "moba":
                onehot = jnp.where(lane == n, 1.0, 0.0).astype(BF16)
            for idx, (h, mi) in enumerate(maps):
                hs = slice(h * LANES, (h + 1) * LANES)
                kn = k_ref[pl.ds(start, blk), hs]
                if mode == "moba":
                    kn = jnp.concatenate([kn, onehot], axis=1)
                s = _qk(qs_ref[idx, rows(side), :], kn) * (scale * LOG2E) + bias_ref[h, tile_dist]
                s_ref[idx, t] = s
                part = s[:, :LANES]
                for r in range(1, reps):
                    part = jnp.maximum(part, s[:, r * LANES:(r + 1) * LANES])
                mx_ref[idx, rows(side), :] = jnp.maximum(mx_ref[idx, rows(side), :], part)

        run_steps(logits_step)

        for side in range(2):
            for idx in range(len(maps)):
                mx_ref[idx, rows(side), :] = jnp.broadcast_to(
                    jnp.max(mx_ref[idx, rows(side), :], axis=-1, keepdims=True), (blk, LANES))
                acc_ref[idx, rows(side), :] = jnp.zeros((blk, 2 * LANES), F32)

        ones = jnp.ones((blk, LANES), BF16)

        def value_step(t, maps=maps):
            side, n, _ = step_info(t)
            start = pl.multiple_of(n * blk, blk)
            for idx, (h, mi) in enumerate(maps):
                hs = slice(h * LANES, (h + 1) * LANES)
                s = s_ref[idx, t]
                mb = mx_ref[idx, rows(side), :]
                p = jnp.concatenate([jnp.exp2(s[:, r * LANES:(r + 1) * LANES] - mb) for r in range(reps)],
                                    axis=1).astype(BF16)
                v_aug = jnp.concatenate([v_ref[pl.ds(start, blk), hs], ones], axis=1)
                acc_ref[idx, rows(side), :] += jnp.dot(p, v_aug, preferred_element_type=F32)

        run_steps(value_step)

        def normalised(idx, side):
            acc = acc_ref[idx, rows(side), :]
            return acc[:, :LANES] / acc[:, LANES:]

        for side in range(2):
            for h in range(g0, g0 + group):
                hs = slice(h * LANES, (h + 1) * LANES)
                i0 = (h - g0) * n_maps
                o = normalised(i0, side)
                if mode == "diff":
                    o = o - p0_ref[...] * normalised(i0 + 1, side)
                    ms = jnp.mean(o * o, axis=-1, keepdims=True)
                    stage_ref[side, :, hs] = (o * lax.rsqrt(ms + EPS) * p1_ref[...]).astype(stage_ref.dtype)
                else:
                    out_ref[side, :, hs] = o

    copies = []
    for side in range(2):
        if mode == "moba":
            o = out_ref[side]
            ms = jnp.mean(o * o, axis=-1, keepdims=True)
            stage_ref[side] = (o * lax.rsqrt(ms + EPS) * p1_ref[...]).astype(stage_ref.dtype)
        row0 = pl.multiple_of((bi * nq + owns[side]) * blk, blk)
        copies.append(pltpu.make_async_copy(stage_ref.at[side], o_hbm.at[pl.ds(row0, blk), :], sem.at[side]))
        copies[-1].start()
    for cp in copies:
        cp.wait()


def attention(proj, bsz, seq, col_q, col_k, col_v, bias, p0, p1, mode, scale, heads):
    blk = ATT_BLOCK
    nq = seq // blk
    width = heads * LANES
    assert width == GROUP_WIDTH and seq % blk == 0 and nq % 2 == 0
    nb = seq // blk
    assert nb <= GATE_ROWS
    n_maps = 2 if mode == "diff" else 1
    group = ATT_MAPS_PER_PASS // n_maps
    const2 = lambda b, i: (0, 0)
    once = pl.Buffered(1)
    return pl.pallas_call(
        functools.partial(_attn_kernel, mode=mode, scale=scale, heads=heads, group=group, nq=nq),
        grid=(bsz, nq // 2),
        in_specs=[
            pl.BlockSpec((blk, width), lambda b, i: (b * nq + i, col_q)),
            pl.BlockSpec((blk, width), lambda b, i: (b * nq + nq - 1 - i, col_q)),
            pl.BlockSpec((seq, width), lambda b, i: (b, col_k)),
            pl.BlockSpec((seq, width), lambda b, i: (b, col_v)),
            pl.BlockSpec((heads, 3, blk, blk), lambda b, i: (0, 0, 0, 0), pipeline_mode=once),
            pl.BlockSpec(p0.shape, const2, pipeline_mode=once),
            pl.BlockSpec(p1.shape, const2, pipeline_mode=once),
        ],
        out_specs=pl.BlockSpec(memory_space=pl.ANY),
        out_shape=jax.ShapeDtypeStruct((bsz * seq, width), BF16),
        scratch_shapes=[
            pltpu.VMEM((ATT_MAPS_PER_PASS, 2 * blk, LANES * (1 if mode == "diff" else 2)), BF16),
            pltpu.VMEM((ATT_MAPS_PER_PASS, nq + 1, blk, blk), F32),
            pltpu.VMEM((ATT_MAPS_PER_PASS, 2 * blk, LANES), F32),
            pltpu.VMEM((ATT_MAPS_PER_PASS, 2 * blk, 2 * LANES), F32),
            pltpu.VMEM((heads, GATE_ROWS, LANES), F32),
            pltpu.VMEM((2, blk, width), F32),
            pltpu.VMEM((2, blk, width), BF16),
            pltpu.SemaphoreType.DMA((2,)),
        ],
        compiler_params=_cparams(("parallel", "arbitrary")),
        name="attn_" + mode,
    )(proj, proj, proj, proj, bias, p0, p1)


def _softplus(x):
    return jnp.maximum(x, 0.0) + jnp.log(1.0 + jnp.exp(-jnp.abs(x)))


def _silu(x):
    return x * jax.nn.sigmoid(x)


def _ssd_kernel(z_ref, xbc_ref, dt_ref, dtt_ref, cw_ref, cb_ref, dtb_ref, dtbt_ref, a_ref, at_ref, dsk_ref,
                nw_ref, o_ref, ext_ref, st_ref, y_ref, *, lc):
    halo = SSD_HALO
    c = pl.program_id(1)

    @pl.when(c == 0)
    def _():
        ext_ref[0:halo, :] = jnp.zeros((halo, SSD_XBC), F32)
        st_ref[...] = jnp.zeros_like(st_ref)

    ext_ref[halo:halo + lc, :] = xbc_ref[...].astype(F32)
    conv = cb_ref[...] + cw_ref[SSD_CONV - 1:SSD_CONV, :] * ext_ref[halo:halo + lc, :]
    for kk in range(1, SSD_CONV):
        conv = conv + cw_ref[SSD_CONV - 1 - kk:SSD_CONV - kk, :] * ext_ref[halo - kk:halo - kk + lc, :]
    ext_ref[0:halo, :] = ext_ref[lc:lc + halo, :]
    xbc = _silu(conv)
    xs = xbc[:, :GROUP_WIDTH]

    dt_c = _softplus(dt_ref[...] + dtb_ref[...])
    dt_r = _softplus(dtt_ref[...] + dtbt_ref[...])
    a_c = dt_c * a_ref[...]
    a_r = dt_r * at_ref[...]
    row = lax.broadcasted_iota(jnp.int32, (lc, lc), 0)
    colm = lax.broadcasted_iota(jnp.int32, (lc, lc), 1)
    causal = row >= colm
    tri = causal.astype(BF16)
    tri_t = (colm >= row).astype(BF16)
    acum_c = _dot_exact_rhs(tri, a_c)
    acum_r = _dot_exact_lhs(a_r, tri_t)
    alast_c = acum_c[lc - 1:lc, :]

    lane = lax.broadcasted_iota(jnp.int32, (lc, LANES), 1)
    lo_half = lane < SSD_HEAD_DIM
    heads_per_group = SSD_HEADS // SSD_GROUPS
    for g in range(SSD_GROUPS):
        bg = xbc[:, GROUP_WIDTH + g * SSD_STATE:GROUP_WIDTH + (g + 1) * SSD_STATE].astype(BF16)
        cg = xbc[:, GROUP_WIDTH + SSD_BC + g * SSD_STATE:GROUP_WIDTH + SSD_BC + (g + 1) * SSD_STATE].astype(BF16)
        cb = _qk(cg, bg)
        for pr in range(heads_per_group // 2):
            ha = g * heads_per_group + 2 * pr
            hb = ha + 1
            tile = slice(ha * SSD_HEAD_DIM, (ha + 2) * SSD_HEAD_DIM)
            xp = xs[:, tile]

            def per_head(col_a, col_b):
                return jnp.where(lo_half, col_a, col_b)

            xdt = xp * per_head(dt_c[:, ha:ha + 1], dt_c[:, hb:hb + 1])
            ydiag = None
            for hh, keep in ((ha, lo_half), (hb, jnp.logical_not(lo_half))):
                seg = acum_c[:, hh:hh + 1] - acum_r[hh:hh + 1, :]
                decay = jnp.exp(jnp.where(causal, seg, NEG_INF))
                mm = (cb * decay).astype(BF16)
                part = jnp.dot(mm, jnp.where(keep, xdt, 0.0).astype(BF16), preferred_element_type=F32)
                ydiag = part if ydiag is None else ydiag + part
            st = st_ref[ha // 2]
            yoff = _qk(cg, st.astype(BF16)) * per_head(jnp.exp(acum_c[:, ha:ha + 1]), jnp.exp(acum_c[:, hb:hb + 1]))
            y_ref[:, tile] = ydiag + yoff + dsk_ref[:, tile] * xp
            to_end = per_head(jnp.exp(alast_c[:, ha:ha + 1] - acum_c[:, ha:ha + 1]),
                              jnp.exp(alast_c[:, hb:hb + 1] - acum_c[:, hb:hb + 1]))
            xdec_t = jnp.transpose(xdt * to_end).astype(BF16)
            sub = lax.broadcasted_iota(jnp.int32, (LANES, SSD_STATE), 0)
            chunk_decay = jnp.where(sub < SSD_HEAD_DIM, jnp.exp(alast_c[:, ha:ha + 1]),
                                    jnp.exp(alast_c[:, hb:hb + 1]))
            st_ref[ha // 2] = st * chunk_decay + jnp.dot(xdec_t, bg, preferred_element_type=F32)

    y = y_ref[...] * _silu(z_ref[...].astype(F32))
    gw = GROUP_WIDTH // SSD_GROUPS
    for g in range(SSD_GROUPS):
        yg = y[:, g * gw:(g + 1) * gw]
        ms = jnp.mean(yg * yg, axis=-1, keepdims=True)
        o_ref[:, g * gw:(g + 1) * gw] = (yg * lax.rsqrt(ms + EPS) * nw_ref[:, g * gw:(g + 1) * gw]).astype(o_ref.dtype)


def _pad_lanes(v):
    return jnp.pad(v.astype(F32), (0, LANES - v.shape[0])).reshape(1, LANES)


def ssd_mixer(proj, dt_raw, bsz, seq, conv_w, conv_b, dt_bias, a_log, d_skip, norm_w, lc=128):
    nc = seq // lc
    t = bsz * seq
    a = -jnp.exp(a_log.astype(F32))
    dt_t = jnp.transpose(dt_raw[:, :SSD_HEADS])
    col16 = lambda v: jnp.broadcast_to(v.astype(F32)[:, None], (SSD_HEADS, LANES))
    dskip = jnp.repeat(d_skip.astype(F32), SSD_HEAD_DIM).reshape(1, GROUP_WIDTH)
    const2 = lambda b, c: (0, 0)
    return pl.pallas_call(
        functools.partial(_ssd_kernel, lc=lc),
        grid=(bsz, nc),
        in_specs=[
            pl.BlockSpec((lc, GROUP_WIDTH), lambda b, c: (b * nc + c, COL_Z)),
            pl.BlockSpec((lc, SSD_XBC), lambda b, c: (b * nc + c, COL_XBC)),
            pl.BlockSpec((lc, LANES), lambda b, c: (b * nc + c, 0)),
            pl.BlockSpec((SSD_HEADS, lc), lambda b, c: (0, b * nc + c)),
            pl.BlockSpec((SSD_CONV, SSD_XBC), const2),
            pl.BlockSpec((1, SSD_XBC), const2),
            pl.BlockSpec((1, LANES), const2),
            pl.BlockSpec((SSD_HEADS, LANES), const2),
            pl.BlockSpec((1, LANES), const2),
            pl.BlockSpec((SSD_HEADS, LANES), const2),
            pl.BlockSpec((1, GROUP_WIDTH), const2),
            pl.BlockSpec((1, GROUP_WIDTH), const2),
        ],
        out_specs=pl.BlockSpec((lc, GROUP_WIDTH), lambda b, c: (b * nc + c, 0)),
        out_shape=jax.ShapeDtypeStruct((t, GROUP_WIDTH), BF16),
        scratch_shapes=[
            pltpu.VMEM((lc + SSD_HALO, SSD_XBC), F32),
            pltpu.VMEM((SSD_HEADS // 2, 2 * SSD_HEAD_DIM, SSD_STATE), F32),
            pltpu.VMEM((lc, GROUP_WIDTH), F32),
        ],
        compiler_params=_cparams(("parallel", "arbitrary")),
        name="ssd_mixer",
    )(proj, proj, dt_raw, dt_t, conv_w.reshape(SSD_CONV, SSD_XBC).astype(F32),
      conv_b.reshape(1, SSD_XBC).astype(F32), _pad_lanes(dt_bias), col16(dt_bias), _pad_lanes(a), col16(a),
      dskip, norm_w.reshape(1, GROUP_WIDTH).astype(F32))


def kernel(x, rel_bias_table, attn_norm_w, w_in, s5_lam_re, s5_lam_im, s5_log_dt, s5_b_re, s5_b_im, s5_c_re, s5_c_im, s5_d, s5_w_glu, s5_out_norm_w, diff_lam_q1, diff_lam_k1, diff_lam_q2, diff_lam_k2, diff_subln_w, moba_out_norm_w, ssd_conv_w, ssd_conv_b, ssd_dt_bias, ssd_a_log, ssd_d, ssd_norm_w, w_out, mlp_norm_w, w_up, w_down, final_norm_w):
    bsz, seq, d_model = x.shape
    depth = w_in.shape[0]
    t = bsz * seq
    x = x.reshape(t, d_model).astype(F32)

    w_in_t = jnp.swapaxes(w_in, 1, 2)
    w_dt_b = jnp.pad(w_in[:, :, PROJ_MAIN:], ((0, 0), (0, 0), (0, LANES - SSD_HEADS))).astype(BF16)

    tbl = rel_bias_table.astype(F32).T
    bias_diff = _bias_tiles(tbl[:DIFF_HEADS], ATT_BLOCK)
    bias_moba = _bias_tiles(tbl[DIFF_HEADS:], ATT_BLOCK)

    def row_scale(ssq):
        r = lax.rsqrt(jnp.sum(ssq, axis=(0, 2)) * (1.0 / d_model) + EPS)
        return jnp.broadcast_to(r[:, None], (t, LANES))

    h, scale = rmsnorm(x, attn_norm_w[0], BF16), None
    for l in range(depth):
        proj = matmul_wstat([h], w_in_t, l, PROJ_MAIN, BF16, row_scale=scale, w_transposed=True)
        dt_raw = matmul(h, w_dt_b, l, LANES, F32, row_scale=scale, tk=d_model)

        y_s5 = s5_mixer(proj, bsz, seq, s5_lam_re[l], s5_lam_im[l], s5_log_dt[l], s5_b_re[l], s5_b_im[l],
                        s5_c_re[l], s5_c_im[l], s5_d[l], s5_w_glu[l], s5_out_norm_w[l])

        lam_init = 0.8 - 0.6 * math.exp(-0.3 * l)
        lam = (jnp.exp(jnp.sum(diff_lam_q1[l].astype(F32) * diff_lam_k1[l].astype(F32)))
               - jnp.exp(jnp.sum(diff_lam_q2[l].astype(F32) * diff_lam_k2[l].astype(F32))) + lam_init)
        y_diff = attention(proj, bsz, seq, COL_DQ, COL_DK, COL_DV, bias_diff,
                           jnp.broadcast_to(lam, (1, LANES)).astype(F32),
                           (diff_subln_w[l].astype(F32) * (1.0 - lam_init)).reshape(1, LANES),
                           "diff", DIFF_HEAD_DIM ** -0.5, DIFF_HEADS)
        y_moba = attention(proj, bsz, seq, COL_MQ, COL_MK, COL_MV, bias_moba,
                           jnp.zeros((1, LANES), F32),
                           moba_out_norm_w[l].astype(F32).reshape(1, GROUP_WIDTH),
                           "moba", (GROUP_WIDTH // MOBA_HEADS) ** -0.5, MOBA_HEADS)
        y_ssd = ssd_mixer(proj, dt_raw, bsz, seq, ssd_conv_w[l], ssd_conv_b[l], ssd_dt_bias[l], ssd_a_log[l],
                          ssd_d[l], ssd_norm_w[l])

        x, h, ssq = matmul_wstat([y_s5, y_diff, y_moba, y_ssd], w_out, l, d_model, F32, residual=x,
                                 norm_w=mlp_norm_w[l])
        u, w_down_b = matmul_wstat([h], w_up, l, w_up.shape[2], BF16, act="relu2", row_scale=row_scale(ssq),
                                   cast_through=w_down)
        if l + 1 < depth:
            x, h, ssq = matmul(u, w_down_b[None], 0, d_model, F32, residual=x, norm_w=attn_norm_w[l + 1])
            scale = row_scale(ssq)
        else:
            x = matmul(u, w_down_b[None], 0, d_model, F32, residual=x)

    return rmsnorm(x, final_norm_w, F32).reshape(bsz, seq, d_model)
```
